```python
import math
import jax, jax.numpy as jnp
from jax import lax
import numpy as np


D_MODEL = 1024
BATCH = 8
SEQ = 2048
DEPTH = 4

N_MIXERS = 2
N_META = 16
D_FF = 2816
CONV_WIDTH = 3
HEAD_DIM = 64
V_DIM = 2 * HEAD_DIM
N_HEADS = D_MODEL // (2 * HEAD_DIM)
ROT_DIM = HEAD_DIM // 4
ROPE_THETA = 500000.0
Q_BLOCK = 128
FRONT_PAD = Q_BLOCK - N_META
N_CONV_LAYERS = (DEPTH + N_MIXERS - 1) // N_MIXERS
N_ATTN_LAYERS = DEPTH // N_MIXERS
NORM_EPS = 1e-6
SUBLN_EPS = 1e-5
NEG_INF = -1e30

kernel_name = 'hybrid_conv_diffattn_macaron_trunk'


def rms_norm(t, g, eps=NORM_EPS):
    t32 = t.astype(jnp.float32)
    t32 = t32 * lax.rsqrt(jnp.mean(t32 * t32, axis=-1, keepdims=True) + eps)
    return t32.astype(t.dtype) * g.astype(t.dtype)


def swiglu(h, w_gu, w_down):
    g, u = jnp.split(h @ w_gu, 2, axis=-1)
    return (jax.nn.silu(g) * u) @ w_down


def short_conv_mixer(h, w_in, conv_w, w_out):
    b, c, xt = jnp.split(h @ w_in, 3, axis=-1)
    u = c * xt
    u = lax.conv_general_dilated(
        u, conv_w[:, None, :].astype(u.dtype),
        window_strides=(1,), padding=((CONV_WIDTH - 1, 0),),
        dimension_numbers=('NWC', 'WIO', 'NWC'),
        feature_group_count=u.shape[-1])
    return (b * u) @ w_out


def partial_rope(t, cos, sin):
    half = ROT_DIM // 2
    t1 = t[..., :half]
    t2 = t[..., half:ROT_DIM]
    c = cos[None, :, None, None, :]
    s = sin[None, :, None, None, :]
    return jnp.concatenate([t1 * c - t2 * s, t1 * s + t2 * c, t[..., ROT_DIM:]], axis=-1)


def diff_attention_mixer(h, w_qkv, lam_p, subln_g, w_o, lambda_init):
    bsz, seq_len, d = h.shape
    dt = h.dtype
    q, k, v = jnp.split(h @ w_qkv, 3, axis=-1)
    q = q.reshape(bsz, seq_len, N_HEADS, 2, HEAD_DIM)
    k = k.reshape(bsz, seq_len, N_HEADS, 2, HEAD_DIM)
    v = v.reshape(bsz, seq_len, N_HEADS, V_DIM)
    pos = jnp.arange(seq_len, dtype=jnp.float32)
    inv_freq = ROPE_THETA ** (-jnp.arange(0, ROT_DIM, 2, dtype=jnp.float32) / ROT_DIM)
    ang = pos[:, None] * inv_freq[None, :]
    cos = jnp.cos(ang).astype(dt)
    sin = jnp.sin(ang).astype(dt)
    q = partial_rope(q, cos, sin)
    k = partial_rope(k, cos, sin)
    q = jnp.pad(q, ((0, 0), (FRONT_PAD, 0), (0, 0), (0, 0), (0, 0))).transpose(0, 2, 3, 1, 4)
    k = jnp.pad(k, ((0, 0), (FRONT_PAD, 0), (0, 0), (0, 0), (0, 0))).transpose(0, 2, 3, 1, 4)
    v = jnp.pad(v, ((0, 0), (FRONT_PAD, 0), (0, 0), (0, 0))).transpose(0, 2, 1, 3)
    lp = seq_len + FRONT_PAD
    lam32 = lam_p.astype(jnp.float32)
    lam = (jnp.exp(jnp.sum(lam32[0] * lam32[1])) - jnp.exp(jnp.sum(lam32[2] * lam32[3]))
           + lambda_init)
    scale = HEAD_DIM ** -0.5
    outs = []
    for j in range(lp // Q_BLOCK):
        q0, q1 = j * Q_BLOCK, (j + 1) * Q_BLOCK
        qb = q[:, :, :, q0:q1]
        kb = k[:, :, :, :q1]
        vb = v[:, :, :q1]
        s = jnp.einsum('bhcqd,bhckd->bhcqk', qb, kb).astype(jnp.float32) * scale
        qi = jnp.arange(q0, q1)[:, None]
        ki = jnp.arange(q1)[None, :]
        mask = (ki <= qi) & (ki >= FRONT_PAD)
        p = jax.nn.softmax(jnp.where(mask, s, NEG_INF), axis=-1)
        a = p[:, :, 0] - lam * p[:, :, 1]
        outs.append(jnp.einsum('bhqk,bhkv->bhqv', a.astype(dt), vb))
    o = jnp.concatenate(outs, axis=2)[:, :, FRONT_PAD:]
    o = rms_norm(o, subln_g, SUBLN_EPS) * (1.0 - lambda_init)
    o = o.transpose(0, 2, 1, 3).reshape(bsz, seq_len, d)
    return o @ w_o


def setup_inputs(seed: int = 0) -> dict:
    key = jax.random.key(seed)
    ks = jax.random.split(key, 20)

    def nrm(k, shape, scale):
        return jax.random.normal(k, shape, jnp.float32) * scale

    def gains(k, shape):
        return 1.0 + nrm(k, shape, 0.02)

    D = D_MODEL
    return {
        'x': nrm(ks[0], (BATCH, SEQ, D), 1.0),
        'meta_tokens': nrm(ks[1], (N_META, D), 1.0),
        'ln_ffn1': gains(ks[2], (DEPTH, 2, D)),
        'ffn1_w_gu': nrm(ks[3], (DEPTH, D, 2 * D_FF), D ** -0.5),
        'ffn1_w_down': nrm(ks[4], (DEPTH, D_FF, D), D_FF ** -0.5),
        'ln_mix': gains(ks[5], (DEPTH, 2, D)),
        'conv_w_in': nrm(ks[6], (N_CONV_LAYERS, D, 3 * D), D ** -0.5),
        'conv_w': nrm(ks[7], (N_CONV_LAYERS, CONV_WIDTH, D), CONV_WIDTH ** -0.5),
        'conv_w_out': nrm(ks[8], (N_CONV_LAYERS, D, D), D ** -0.5),
        'attn_w_qkv': nrm(ks[9], (N_ATTN_LAYERS, D, 3 * D), D ** -0.5),
        'attn_lambda': nrm(ks[10], (N_ATTN_LAYERS, 4, HEAD_DIM), 0.1),
        'attn_subln_g': gains(ks[11], (N_ATTN_LAYERS, V_DIM)),
        'attn_w_o': nrm(ks[12], (N_ATTN_LAYERS, D, D), D ** -0.5),
        'ln_ffn2': gains(ks[13], (DEPTH, 2, D)),
        'ffn2_w_gu': nrm(ks[14], (DEPTH, D, 2 * D_FF), D ** -0.5),
        'ffn2_w_down': nrm(ks[15], (DEPTH, D_FF, D), D_FF ** -0.5),
    }


def reference(x, meta_tokens, ln_ffn1, ffn1_w_gu, ffn1_w_down, ln_mix,
              conv_w_in, conv_w, conv_w_out, attn_w_qkv, attn_lambda,
              attn_subln_g, attn_w_o, ln_ffn2, ffn2_w_gu, ffn2_w_down):
    bsz = x.shape[0]
    meta = jnp.broadcast_to(meta_tokens.astype(x.dtype)[None], (bsz, N_META, x.shape[-1]))
    h = jnp.concatenate([meta, x], axis=1)
    for i in range(DEPTH):
        f = swiglu(rms_norm(h, ln_ffn1[i, 0]), ffn1_w_gu[i], ffn1_w_down[i])
        h = h + 0.5 * rms_norm(f, ln_ffn1[i, 1])
        hn = rms_norm(h, ln_mix[i, 0])
        j = i // N_MIXERS
        if i % N_MIXERS == 0:
            m = short_conv_mixer(hn, conv_w_in[j], conv_w[j], conv_w_out[j])
        else:
            lambda_init = 0.8 - 0.6 * math.exp(-0.3 * i)
            m = diff_attention_mixer(hn, attn_w_qkv[j], attn_lambda[j],
                                     attn_subln_g[j], attn_w_o[j], lambda_init)
        h = h + rms_norm(m, ln_mix[i, 1])
        f = swiglu(rms_norm(h, ln_ffn2[i, 0]), ffn2_w_gu[i], ffn2_w_down[i])
        h = h + 0.5 * rms_norm(f, ln_ffn2[i, 1])
    return h[:, N_META:]
```

```python
import functools
import math

import jax
import jax.numpy as jnp
from jax import lax
from jax.experimental import pallas as pl
from jax.experimental.pallas import tpu as pltpu

D_MODEL = 1024
DEPTH = 4
N_MIXERS = 2
N_META = 16
D_FF = 2816
CONV_WIDTH = 3
HEAD_DIM = 64
V_DIM = 2 * HEAD_DIM
N_HEADS = D_MODEL // V_DIM
ROT_DIM = HEAD_DIM // 4
ROPE_THETA = 500000.0
NORM_EPS = 1e-6
SUBLN_EPS = 1e-5
NEG_INF = -1e30

F32 = jnp.float32
BF16 = jnp.bfloat16

V7X_VMEM_LIMIT_BYTES = 60 * 1024 * 1024
SUBLANES = 8
LANES = 128
CARRY_ROWS = SUBLANES


def _rms(t, g, eps):
    return t * lax.rsqrt(jnp.mean(t * t, axis=-1, keepdims=True) + eps) * g


def _resident(shape):
    zeros = (0,) * len(shape)
    return pl.BlockSpec(shape, lambda *_: zeros, pipeline_mode=pl.Buffered(1))


def _params(n_grid_dims):
    return pltpu.CompilerParams(
        dimension_semantics=("arbitrary",) * n_grid_dims,
        vmem_limit_bytes=V7X_VMEM_LIMIT_BYTES)


def _ffn_kernel(h_ref, g_ref, wgu_ref, wdn_ref, o_ref):
    h = h_ref[...]
    g = g_ref[...]
    hn = _rms(h, g[0:1], NORM_EPS).astype(BF16)
    gu = jnp.dot(hn, wgu_ref[...], preferred_element_type=F32)
    gate = gu[:, :D_FF]
    up = gu[:, D_FF:]
    act = (gate * jax.nn.sigmoid(gate) * up).astype(BF16)
    f = jnp.dot(act, wdn_ref[...], preferred_element_type=F32)
    o_ref[...] = h + 0.5 * _rms(f, g[1:2], NORM_EPS)


def _ffn(h, g, wgu, wdn, tm):
    rows = h.shape[0]
    return pl.pallas_call(
        _ffn_kernel,
        grid=(rows // tm,),
        in_specs=[
            pl.BlockSpec((tm, D_MODEL), lambda i: (i, 0)),
            _resident((2, D_MODEL)),
            _resident((D_MODEL, 2 * D_FF)),
            _resident((D_FF, D_MODEL)),
        ],
        out_specs=pl.BlockSpec((tm, D_MODEL), lambda i: (i, 0)),
        out_shape=jax.ShapeDtypeStruct((rows, D_MODEL), F32),
        compiler_params=_params(1),
        name="ffn",
    )(h, g, wgu, wdn)


def _conv_kernel(h_ref, g_ref, tail_in_ref, win_ref, cw_ref, wout_ref,
                 o_ref, tail_out_ref, carry_ref, *, tm):
    @pl.when(pl.program_id(1) == 0)
    def _():
        carry_ref[...] = tail_in_ref[...]

    h = h_ref[0]
    g = g_ref[...]
    hn = _rms(h, g[0:1], NORM_EPS).astype(BF16)
    bcx = jnp.dot(hn, win_ref[...], preferred_element_type=F32)
    b = bcx[:, :D_MODEL]
    u = bcx[:, D_MODEL:2 * D_MODEL] * bcx[:, 2 * D_MODEL:]
    prev = carry_ref[...]
    row = lax.broadcasted_iota(jnp.int32, (tm, 1), 0)
    last = prev[CARRY_ROWS - 1:CARRY_ROWS]
    u1 = jnp.where(row == 0, last, pltpu.roll(u, 1, 0))
    u2 = jnp.where(row == 0, prev[CARRY_ROWS - 2:CARRY_ROWS - 1],
                   jnp.where(row == 1, last, pltpu.roll(u, 2, 0)))
    cw = cw_ref[...]
    conv = cw[0:1] * u2 + cw[1:2] * u1 + cw[2:3] * u
    tail = u[tm - CARRY_ROWS:]
    carry_ref[...] = tail
    tail_out_ref[0] = tail
    y = (b * conv).astype(BF16)
    m = jnp.dot(y, wout_ref[...], preferred_element_type=F32)
    o_ref[0] = h + _rms(m, g[1:2], NORM_EPS)


def _conv_mixer(h, g, tail_in, win, cw, wout, tm):
    bsz, seq, _ = h.shape
    return pl.pallas_call(
        functools.partial(_conv_kernel, tm=tm),
        grid=(bsz, seq // tm),
        in_specs=[
            pl.BlockSpec((1, tm, D_MODEL), lambda b, j: (b, j, 0)),
            _resident((2, D_MODEL)),
            _resident((CARRY_ROWS, D_MODEL)),
            _resident((D_MODEL, 3 * D_MODEL)),
            _resident((CONV_WIDTH, D_MODEL)),
            _resident((D_MODEL, D_MODEL)),
        ],
        out_specs=[
            pl.BlockSpec((1, tm, D_MODEL), lambda b, j: (b, j, 0)),
            pl.BlockSpec((1, CARRY_ROWS, D_MODEL), lambda b, j: (b, 0, 0)),
        ],
        out_shape=[
            jax.ShapeDtypeStruct((bsz, seq, D_MODEL), F32),
            jax.ShapeDtypeStruct((bsz, CARRY_ROWS, D_MODEL), F32),
        ],
        scratch_shapes=[pltpu.VMEM((CARRY_ROWS, D_MODEL), F32)],
        compiler_params=_params(2),
        name="conv_mixer",
    )(h, g, tail_in, win, cw, wout)


def _qkv_kernel(h_ref, g_ref, w_ref, cos_ref, sin_lo_ref, sin_hi_ref, qkv_ref):
    g = g_ref[...]
    hn = _rms(h_ref[0], g[0:1], NORM_EPS).astype(BF16)
    qkv = jnp.dot(hn, w_ref[...], preferred_element_type=F32)
    cos = cos_ref[...]
    sin_lo = sin_lo_ref[...]
    sin_hi = sin_hi_ref[...]
    half = ROT_DIM // 2
    for blk in range(2 * N_HEADS):
        x = qkv[:, blk * LANES:(blk + 1) * LANES]
        r = x * cos + pltpu.roll(x, LANES - half, 1) * sin_lo + pltpu.roll(x, half, 1) * sin_hi
        if blk < N_HEADS:
            r = r * (HEAD_DIM ** -0.5)
        qkv_ref[0, :, blk * LANES:(blk + 1) * LANES] = r.astype(BF16)
    qkv_ref[0, :, 2 * D_MODEL:] = qkv[:, 2 * D_MODEL:].astype(BF16)


def _qkv_proj(h, g, w, cos, sin_lo, sin_hi, tm):
    bsz, seq, _ = h.shape
    table = pl.BlockSpec((tm, LANES), lambda b, j: (j, 0))
    return pl.pallas_call(
        _qkv_kernel,
        grid=(bsz, seq // tm),
        in_specs=[
            pl.BlockSpec((1, tm, D_MODEL), lambda b, j: (b, j, 0)),
            _resident((2, D_MODEL)),
            _resident((D_MODEL, 3 * D_MODEL)),
            table, table, table,
        ],
        out_specs=pl.BlockSpec((1, tm, 3 * D_MODEL), lambda b, j: (b, j, 0)),
        out_shape=jax.ShapeDtypeStruct((bsz, seq, 3 * D_MODEL), BF16),
        compiler_params=_params(2),
        name="qkv_proj",
    )(h, g, w, cos, sin_lo, sin_hi)


def _attn_kernel(*refs, tq, lambda_init, has_prefix):
    if has_prefix:
        lam_ref, sg_ref, q_ref, k_ref, v_ref, kp_ref, vp_ref, o_ref = refs
    else:
        lam_ref, sg_ref, q_ref, k_ref, v_ref, o_ref = refs
    qi = pl.program_id(2)
    q = q_ref[0]
    lane = lax.broadcasted_iota(jnp.int32, (1, LANES), 1)
    zero = jnp.zeros_like(q)
    qq = jnp.concatenate([jnp.where(lane < HEAD_DIM, q, zero),
                          jnp.where(lane >= HEAD_DIM, q, zero)], axis=0)
    nt = (((1,), (1,)), ((), ()))

    def fold(carry, k, v, mask):
        m, l, acc = carry
        s = lax.dot_general(qq, k, nt, preferred_element_type=F32)
        if mask is not None:
            s = jnp.where(mask, s, NEG_INF)
        m_new = jnp.maximum(m, jnp.max(s, axis=1, keepdims=True))
        alpha = jnp.exp(m - m_new)
        p = jnp.exp(s - m_new)
        l = alpha * l + jnp.sum(p, axis=1, keepdims=True)
        acc = alpha * acc + jnp.dot(p.astype(BF16), v, preferred_element_type=F32)
        return m_new, l, acc

    carry = (jnp.full((2 * tq, 1), NEG_INF, F32),
             jnp.zeros((2 * tq, 1), F32),
             jnp.zeros((2 * tq, V_DIM), F32))
    if has_prefix:
        carry = fold(carry, kp_ref[...], vp_ref[...], None)

    def block(kb, c):
        start = pl.multiple_of(kb * tq, tq)
        return fold(c, k_ref[0, pl.ds(start, tq), :], v_ref[0, pl.ds(start, tq), :], None)

    carry = lax.fori_loop(0, qi, block, carry)
    start = pl.multiple_of(qi * tq, tq)
    r = lax.broadcasted_iota(jnp.int32, (2 * tq, tq), 0)
    c = lax.broadcasted_iota(jnp.int32, (2 * tq, tq), 1)
    causal = c <= jnp.where(r >= tq, r - tq, r)
    _, l, acc = fold(carry, k_ref[0, pl.ds(start, tq), :], v_ref[0, pl.ds(start, tq), :], causal)

    lam_p = lam_ref[...]
    lam = (jnp.exp(jnp.sum(lam_p[0:1] * lam_p[1:2], axis=1, keepdims=True))
           - jnp.exp(jnp.sum(lam_p[2:3] * lam_p[3:4], axis=1, keepdims=True)) + lambda_init)
    pv = acc / l
    o = pv[:tq] - lam * pv[tq:]
    o = _rms(o, sg_ref[...], SUBLN_EPS) * (1.0 - lambda_init)
    o_ref[0] = o.astype(BF16)


def _attention(qkv, lam_p, subln_g, prefix_kv, tq, lambda_init):
    bsz, seq, _ = qkv.shape
    has_prefix = prefix_kv is not None
    in_specs = [
        _resident((4, HEAD_DIM)),
        _resident((1, V_DIM)),
        pl.BlockSpec((1, tq, LANES), lambda b, h, i: (b, i, h)),
        pl.BlockSpec((1, seq, LANES), lambda b, h, i: (b, 0, N_HEADS + h)),
        pl.BlockSpec((1, seq, LANES), lambda b, h, i: (b, 0, 2 * N_HEADS + h)),
    ]
    args = [lam_p, subln_g.reshape(1, V_DIM), qkv, qkv, qkv]
    if has_prefix:
        n_prefix = prefix_kv.shape[0]
        in_specs += [
            pl.BlockSpec((n_prefix, LANES), lambda b, h, i: (0, N_HEADS + h)),
            pl.BlockSpec((n_prefix, LANES), lambda b, h, i: (0, 2 * N_HEADS + h)),
        ]
        args += [prefix_kv, prefix_kv]
    return pl.pallas_call(
        functools.partial(_attn_kernel, tq=tq, lambda_init=lambda_init, has_prefix=has_prefix),
        grid=(bsz, N_HEADS, seq // tq),
        in_specs=in_specs,
        out_specs=pl.BlockSpec((1, tq, LANES), lambda b, h, i: (b, i, h)),
        out_shape=jax.ShapeDtypeStruct((bsz, seq, D_MODEL), BF16),
        compiler_params=_params(3),
        name="diff_attention",
    )(*args)


def _oproj_kernel(h_ref, g_ref, a_ref, w_ref, o_ref):
    m = jnp.dot(a_ref[...], w_ref[...], preferred_element_type=F32)
    o_ref[...] = h_ref[...] + _rms(m, g_ref[...][1:2], NORM_EPS)


def _oproj(h, g, a, w, tm):
    rows = h.shape[0]
    tile = pl.BlockSpec((tm, D_MODEL), lambda i: (i, 0))
    return pl.pallas_call(
        _oproj_kernel,
        grid=(rows // tm,),
        in_specs=[tile, _resident((2, D_MODEL)), tile, _resident((D_MODEL, D_MODEL))],
        out_specs=tile,
        out_shape=jax.ShapeDtypeStruct((rows, D_MODEL), F32),
        compiler_params=_params(1),
        name="attn_out_proj",
    )(h, g, a, w)


def _rope_tables(first_pos, n_pos):
    half = ROT_DIM // 2
    pos = jnp.arange(first_pos, first_pos + n_pos, dtype=F32)
    inv_freq = ROPE_THETA ** (-jnp.arange(0, ROT_DIM, 2, dtype=F32) / ROT_DIM)
    ang = pos[:, None] * inv_freq[None, :]
    cos, sin = jnp.cos(ang), jnp.sin(ang)
    ones = jnp.ones((n_pos, HEAD_DIM - ROT_DIM), F32)
    zeros = jnp.zeros((n_pos, HEAD_DIM - half), F32)
    cos64 = jnp.concatenate([cos, cos, ones], axis=1)
    lo64 = jnp.concatenate([-sin, zeros], axis=1)
    hi64 = jnp.concatenate([jnp.zeros((n_pos, half), F32), sin, ones * 0.0], axis=1)
    return tuple(jnp.concatenate([t, t], axis=1) for t in (cos64, lo64, hi64))


def _attn_mixer(h, g, wqkv, lam_p, subln_g, wo, first_pos, prefix_kv, tm, tq, lambda_init):
    bsz, seq, _ = h.shape
    qkv = _qkv_proj(h, g, wqkv, *_rope_tables(first_pos, seq), tm)
    a = _attention(qkv, lam_p, subln_g, prefix_kv, tq, lambda_init)
    out = _oproj(h.reshape(bsz * seq, D_MODEL), g, a.reshape(bsz * seq, D_MODEL), wo, tm)
    return out.reshape(bsz, seq, D_MODEL), qkv


ROW_TILE = 512
Q_TILE = 256


def kernel(x, meta_tokens, ln_ffn1, ffn1_w_gu, ffn1_w_down, ln_mix, conv_w_in, conv_w, conv_w_out,
           attn_w_qkv, attn_lambda, attn_subln_g, attn_w_o, ln_ffn2, ffn2_w_gu, ffn2_w_down):
    bsz, seq, d = x.shape
    assert d == D_MODEL and seq % ROW_TILE == 0 and seq % Q_TILE == 0
    bf = lambda w: w.astype(BF16)
    ffn1_w_gu, ffn1_w_down, ffn2_w_gu, ffn2_w_down = map(bf, (ffn1_w_gu, ffn1_w_down, ffn2_w_gu, ffn2_w_down))
    conv_w_in, conv_w_out, attn_w_qkv, attn_w_o = map(bf, (conv_w_in, conv_w_out, attn_w_qkv, attn_w_o))

    hx = x
    hm = meta_tokens.astype(x.dtype)[None]

    def ffn_both(hx, hm, g, wgu, wdn):
        hx = _ffn(hx.reshape(bsz * seq, D_MODEL), g, wgu, wdn, ROW_TILE).reshape(bsz, seq, D_MODEL)
        hm = _ffn(hm.reshape(N_META, D_MODEL), g, wgu, wdn, N_META).reshape(1, N_META, D_MODEL)
        return hx, hm

    for i in range(DEPTH):
        hx, hm = ffn_both(hx, hm, ln_ffn1[i], ffn1_w_gu[i], ffn1_w_down[i])
        j = i // N_MIXERS
        if i % N_MIXERS == 0:
            no_history = jnp.zeros((CARRY_ROWS, D_MODEL), F32)
            hm, meta_tail = _conv_mixer(hm, ln_mix[i], no_history, conv_w_in[j], conv_w[j], conv_w_out[j], N_META)
            hx, _ = _conv_mixer(hx, ln_mix[i], meta_tail[0], conv_w_in[j], conv_w[j], conv_w_out[j], ROW_TILE)
        else:
            lambda_init = 0.8 - 0.6 * math.exp(-0.3 * i)
            hm, meta_qkv = _attn_mixer(hm, ln_mix[i], attn_w_qkv[j], attn_lambda[j], attn_subln_g[j],
                                       attn_w_o[j], 0, None, N_META, N_META, lambda_init)
            hx, _ = _attn_mixer(hx, ln_mix[i], attn_w_qkv[j], attn_lambda[j], attn_subln_g[j],
                                attn_w_o[j], N_META, meta_qkv[0], ROW_TILE, Q_TILE, lambda_init)
        hx, hm = ffn_both(hx, hm, ln_ffn2[i], ffn2_w_gu[i], ffn2_w_down[i])
    return hx
```

```python
import functools
import math

import jax
import jax.numpy as jnp
from jax import lax
from jax.experimental import pallas as pl
from jax.experimental.pallas import tpu as pltpu

D_MODEL = 1024
DEPTH = 4
N_MIXERS = 2
N_META = 16
D_FF = 2816
CONV_WIDTH = 3
HEAD_DIM = 64
V_DIM = 2 * HEAD_DIM
N_HEADS = D_MODEL // V_DIM
ROT_DIM = HEAD_DIM // 4
ROPE_THETA = 500000.0
NORM_EPS = 1e-6
SUBLN_EPS = 1e-5
NEG_INF = -1e30

F32 = jnp.float32
BF16 = jnp.bfloat16

V7X_VMEM_LIMIT_BYTES = 60 * 1024 * 1024
SUBLANES = 8
LANES = 128
CARRY_ROWS = SUBLANES


def _rms(t, g, eps):
    return t * lax.rsqrt(jnp.mean(t * t, axis=-1, keepdims=True) + eps) * g


def _resident(shape):
    zeros = (0,) * len(shape)
    return pl.BlockSpec(shape, lambda *_: zeros, pipeline_mode=pl.Buffered(1))


def _params(n_grid_dims):
    return pltpu.CompilerParams(
        dimension_semantics=("arbitrary",) * n_grid_dims,
        vmem_limit_bytes=V7X_VMEM_LIMIT_BYTES)


def _ffn_kernel(h_ref, g_ref, wgu_ref, wdn_ref, o_ref):
    h = h_ref[...]
    g = g_ref[...]
    hn = _rms(h, g[0:1], NORM_EPS).astype(BF16)
    gu = jnp.dot(hn, wgu_ref[...], preferred_element_type=F32)
    gate = gu[:, :D_FF]
    up = gu[:, D_FF:]
    act = (gate * jax.nn.sigmoid(gate) * up).astype(BF16)
    f = jnp.dot(act, wdn_ref[...], preferred_element_type=F32)
    o_ref[...] = h + 0.5 * _rms(f, g[1:2], NORM_EPS)


def _ffn(h, g, wgu, wdn, tm):
    rows = h.shape[0]
    return pl.pallas_call(
        _ffn_kernel,
        grid=(rows // tm,),
        in_specs=[
            pl.BlockSpec((tm, D_MODEL), lambda i: (i, 0)),
            _resident((2, D_MODEL)),
            _resident((D_MODEL, 2 * D_FF)),
            _resident((D_FF, D_MODEL)),
        ],
        out_specs=pl.BlockSpec((tm, D_MODEL), lambda i: (i, 0)),
        out_shape=jax.ShapeDtypeStruct((rows, D_MODEL), F32),
        compiler_params=_params(1),
        name="ffn",
    )(h, g, wgu, wdn)


def _conv_kernel(h_ref, g_ref, tail_in_ref, win_ref, cw_ref, wout_ref,
                 o_ref, tail_out_ref, carry_ref, *, tm):
    @pl.when(pl.program_id(1) == 0)
    def _():
        carry_ref[...] = tail_in_ref[...]

    h = h_ref[0]
    g = g_ref[...]
    hn = _rms(h, g[0:1], NORM_EPS).astype(BF16)
    bcx = jnp.dot(hn, win_ref[...], preferred_element_type=F32)
    b = bcx[:, :D_MODEL]
    u = bcx[:, D_MODEL:2 * D_MODEL] * bcx[:, 2 * D_MODEL:]
    prev = carry_ref[...]
    row = lax.broadcasted_iota(jnp.int32, (tm, 1), 0)
    last = prev[CARRY_ROWS - 1:CARRY_ROWS]
    u1 = jnp.where(row == 0, last, pltpu.roll(u, 1, 0))
    u2 = jnp.where(row == 0, prev[CARRY_ROWS - 2:CARRY_ROWS - 1],
                   jnp.where(row == 1, last, pltpu.roll(u, 2, 0)))
    cw = cw_ref[...]
    conv = cw[0:1] * u2 + cw[1:2] * u1 + cw[2:3] * u
    tail = u[tm - CARRY_ROWS:]
    carry_ref[...] = tail
    tail_out_ref[0] = tail
    y = (b * conv).astype(BF16)
    m = jnp.dot(y, wout_ref[...], preferred_element_type=F32)
    o_ref[0] = h + _rms(m, g[1:2], NORM_EPS)


def _conv_mixer(h, g, tail_in, win, cw, wout, tm):
    bsz, seq, _ = h.shape
    return pl.pallas_call(
        functools.partial(_conv_kernel, tm=tm),
        grid=(bsz, seq // tm),
        in_specs=[
            pl.BlockSpec((1, tm, D_MODEL), lambda b, j: (b, j, 0)),
            _resident((2, D_MODEL)),
            _resident((CARRY_ROWS, D_MODEL)),
            _resident((D_MODEL, 3 * D_MODEL)),
            _resident((CONV_WIDTH, D_MODEL)),
            _resident((D_MODEL, D_MODEL)),
        ],
        out_specs=[
            pl.BlockSpec((1, tm, D_MODEL), lambda b, j: (b, j, 0)),
            pl.BlockSpec((1, CARRY_ROWS, D_MODEL), lambda b, j: (b, 0, 0)),
        ],
        out_shape=[
            jax.ShapeDtypeStruct((bsz, seq, D_MODEL), F32),
            jax.ShapeDtypeStruct((bsz, CARRY_ROWS, D_MODEL), F32),
        ],
        scratch_shapes=[pltpu.VMEM((CARRY_ROWS, D_MODEL), F32)],
        compiler_params=_params(2),
        name="conv_mixer",
    )(h, g, tail_in, win, cw, wout)


NT_DIMS = (((1,), (1,)), ((), ()))


ONES_ROWS = 16
VT_ROWS = V_DIM + ONES_ROWS
Q_SCALE = HEAD_DIM ** -0.5 * math.log2(math.e)


def _qkv_kernel(h_ref, g_ref, wqk_ref, wvt_ref, cos_ref, sin_lo_ref, sin_hi_ref, qk_ref, vt_ref):
    g = g_ref[...]
    hn = _rms(h_ref[0], g[0:1], NORM_EPS).astype(BF16)
    qk = jnp.dot(hn, wqk_ref[...], preferred_element_type=F32)
    cos = cos_ref[...]
    sin_lo = sin_lo_ref[...]
    sin_hi = sin_hi_ref[...]
    half = ROT_DIM // 2
    for blk in range(2 * N_HEADS):
        x = qk[:, blk * LANES:(blk + 1) * LANES]
        r = x * cos + pltpu.roll(x, LANES - half, 1) * sin_lo + pltpu.roll(x, half, 1) * sin_hi
        if blk < N_HEADS:
            r = r * Q_SCALE
        qk_ref[0, :, blk * LANES:(blk + 1) * LANES] = r.astype(BF16)
    vt = lax.dot_general(wvt_ref[...], hn, NT_DIMS, preferred_element_type=F32).astype(BF16)
    ones = jnp.ones((ONES_ROWS, vt.shape[1]), BF16)
    for head in range(N_HEADS):
        vt_ref[0, head, :V_DIM, :] = vt[head * V_DIM:(head + 1) * V_DIM]
        vt_ref[0, head, V_DIM:, :] = ones


def _qkv_proj(h, g, wqk, wvt, cos, sin_lo, sin_hi, tm):
    bsz, seq, _ = h.shape
    table = pl.BlockSpec((tm, LANES), lambda b, j: (j, 0))
    return pl.pallas_call(
        _qkv_kernel,
        grid=(bsz, seq // tm),
        in_specs=[
            pl.BlockSpec((1, tm, D_MODEL), lambda b, j: (b, j, 0)),
            _resident((2, D_MODEL)),
            _resident((D_MODEL, 2 * D_MODEL)),
            _resident((D_MODEL, D_MODEL)),
            table, table, table,
        ],
        out_specs=[
            pl.BlockSpec((1, tm, 2 * D_MODEL), lambda b, j: (b, j, 0)),
            pl.BlockSpec((1, N_HEADS, VT_ROWS, tm), lambda b, j: (b, 0, 0, j)),
        ],
        out_shape=[
            jax.ShapeDtypeStruct((bsz, seq, 2 * D_MODEL), BF16),
            jax.ShapeDtypeStruct((bsz, N_HEADS, VT_ROWS, seq), BF16),
        ],
        compiler_params=_params(2),
        name="qkv_proj",
    )(h, g, wqk, wvt, cos, sin_lo, sin_hi)


def _attn_t_kernel(lam_ref, sg_ref, q_ref, k_ref, vt_ref, kp_ref, vtp_ref, o_ref, acc_ref,
                   *, tq, n_heads, lambda_init):
    seq = q_ref.shape[1]
    lane = lax.broadcasted_iota(jnp.int32, (1, LANES), 1)
    key_i = lax.broadcasted_iota(jnp.int32, (tq, 2 * tq), 0)
    qry_i = lax.broadcasted_iota(jnp.int32, (tq, 2 * tq), 1)
    causal = key_i <= jnp.where(qry_i >= tq, qry_i - tq, qry_i)
    lam_p = lam_ref[...]
    lam = (jnp.exp(jnp.sum(lam_p[0:1] * lam_p[1:2], axis=1, keepdims=True))
           - jnp.exp(jnp.sum(lam_p[2:3] * lam_p[3:4], axis=1, keepdims=True)) + lambda_init)
    sg = sg_ref[...]
    heads = [slice(g * LANES, (g + 1) * LANES) for g in range(n_heads)]

    def scores(k, qq):
        return lax.dot_general(k, qq, NT_DIMS, preferred_element_type=F32)

    def q_tile(qi, _):
        q_start = pl.multiple_of(qi * tq, tq)
        qqs = []
        for hs in heads:
            q = q_ref[0, pl.ds(q_start, tq), hs]
            zero = jnp.zeros_like(q)
            qqs.append(jnp.concatenate([jnp.where(lane < HEAD_DIM, q, zero),
                                        jnp.where(lane >= HEAD_DIM, q, zero)], axis=0))

        sts_p = [scores(kp_ref[:, hs], qqs[g]) for g, hs in enumerate(heads)]
        sts_d = [scores(k_ref[0, pl.ds(q_start, tq), hs], qqs[g]) for g, hs in enumerate(heads)]
        maxes = []
        for g in range(n_heads):
            st_d = jnp.where(causal, sts_d[g], NEG_INF)
            m = jnp.maximum(jnp.max(sts_p[g], axis=0, keepdims=True), jnp.max(st_d, axis=0, keepdims=True))
            p_p = jnp.exp2(sts_p[g] - m).astype(BF16)
            p_d = jnp.exp2(st_d - m).astype(BF16)
            acc_ref[g] = (jnp.dot(vtp_ref[g], p_p, preferred_element_type=F32)
                          + jnp.dot(vt_ref[0, g, :, pl.ds(q_start, tq)], p_d, preferred_element_type=F32))
            maxes.append(m)

        def block(kb, maxes):
            start = pl.multiple_of(kb * tq, tq)
            sts = [scores(k_ref[0, pl.ds(start, tq), hs], qqs[g]) for g, hs in enumerate(heads)]
            new_maxes = []
            for g in range(n_heads):
                m_new = jnp.maximum(maxes[g], jnp.max(sts[g], axis=0, keepdims=True))
                alpha = jnp.exp2(maxes[g] - m_new)
                p = jnp.exp2(sts[g] - m_new).astype(BF16)
                pv = jnp.dot(vt_ref[0, g, :, pl.ds(start, tq)], p, preferred_element_type=F32)
                acc_ref[g] = alpha * acc_ref[g] + pv
                new_maxes.append(m_new)
            return tuple(new_maxes)

        lax.fori_loop(0, qi, block, tuple(maxes))

        for g, hs in enumerate(heads):
            acc = acc_ref[g]
            pv = acc[:V_DIM] * (1.0 / acc[V_DIM:V_DIM + 1])
            o = pv[:, :tq] - lam * pv[:, tq:]
            o = o * lax.rsqrt(jnp.mean(o * o, axis=0, keepdims=True) + SUBLN_EPS) * sg * (1.0 - lambda_init)
            o_ref[0, pl.ds(q_start, tq), hs] = o.T.astype(BF16)
        return 0

    lax.fori_loop(0, seq // tq, q_tile, 0)


def _attention_t(qk, vt, lam_p, subln_g, prefix_qk, prefix_vt, tq, n_heads, lambda_init):
    bsz, seq, _ = qk.shape
    n_prefix = prefix_qk.shape[0]
    groups = N_HEADS // n_heads
    width = n_heads * LANES
    return pl.pallas_call(
        functools.partial(_attn_t_kernel, tq=tq, n_heads=n_heads, lambda_init=lambda_init),
        grid=(bsz, groups),
        in_specs=[
            _resident((4, HEAD_DIM)),
            _resident((V_DIM, 1)),
            pl.BlockSpec((1, seq, width), lambda b, h: (b, 0, h)),
            pl.BlockSpec((1, seq, width), lambda b, h: (b, 0, groups + h)),
            pl.BlockSpec((1, n_heads, VT_ROWS, seq), lambda b, h: (b, h, 0, 0)),
            pl.BlockSpec((n_prefix, width), lambda b, h: (0, groups + h)),
            pl.BlockSpec((n_heads, VT_ROWS, n_prefix), lambda b, h: (h, 0, 0)),
        ],
        out_specs=pl.BlockSpec((1, seq, width), lambda b, h: (b, 0, h)),
        out_shape=jax.ShapeDtypeStruct((bsz, seq, D_MODEL), BF16),
        scratch_shapes=[pltpu.VMEM((n_heads, VT_ROWS, 2 * tq), F32)],
        compiler_params=_params(2),
        name="diff_attention",
    )(lam_p, subln_g.reshape(V_DIM, 1), qk, qk, vt, prefix_qk, prefix_vt)


def _attn_small_kernel(lam_ref, sg_ref, q_ref, k_ref, v_ref, o_ref, *, lambda_init):
    tq = q_ref.shape[0]
    q = q_ref[...]
    lane = lax.broadcasted_iota(jnp.int32, (1, LANES), 1)
    zero = jnp.zeros_like(q)
    qq = jnp.concatenate([jnp.where(lane < HEAD_DIM, q, zero),
                          jnp.where(lane >= HEAD_DIM, q, zero)], axis=0)
    s = lax.dot_general(qq, k_ref[...], NT_DIMS, preferred_element_type=F32)
    r = lax.broadcasted_iota(jnp.int32, (2 * tq, tq), 0)
    c = lax.broadcasted_iota(jnp.int32, (2 * tq, tq), 1)
    s = jnp.where(c <= jnp.where(r >= tq, r - tq, r), s, NEG_INF)
    p = jnp.exp2(s - jnp.max(s, axis=1, keepdims=True))
    l = jnp.sum(p, axis=1, keepdims=True)
    pv = jnp.dot(p.astype(BF16), v_ref[...], preferred_element_type=F32) / l
    lam_p = lam_ref[...]
    lam = (jnp.exp(jnp.sum(lam_p[0:1] * lam_p[1:2], axis=1, keepdims=True))
           - jnp.exp(jnp.sum(lam_p[2:3] * lam_p[3:4], axis=1, keepdims=True)) + lambda_init)
    o = pv[:tq] - lam * pv[tq:]
    o_ref[...] = (_rms(o, sg_ref[...], SUBLN_EPS) * (1.0 - lambda_init)).astype(BF16)


def _attention_small(qk, v, lam_p, subln_g, lambda_init):
    seq = qk.shape[0]
    return pl.pallas_call(
        functools.partial(_attn_small_kernel, lambda_init=lambda_init),
        grid=(N_HEADS,),
        in_specs=[
            _resident((4, HEAD_DIM)),
            _resident((1, V_DIM)),
            pl.BlockSpec((seq, LANES), lambda h: (0, h)),
            pl.BlockSpec((seq, LANES), lambda h: (0, N_HEADS + h)),
            pl.BlockSpec((seq, LANES), lambda h: (0, h)),
        ],
        out_specs=pl.BlockSpec((seq, LANES), lambda h: (0, h)),
        out_shape=jax.ShapeDtypeStruct((seq, D_MODEL), BF16),
        compiler_params=_params(1),
        name="diff_attention_meta",
    )(lam_p, subln_g.reshape(1, V_DIM), qk, qk, v)


def _oproj_kernel(h_ref, g_ref, a_ref, w_ref, o_ref):
    m = jnp.dot(a_ref[...], w_ref[...], preferred_element_type=F32)
    o_ref[...] = h_ref[...] + _rms(m, g_ref[...][1:2], NORM_EPS)


def _oproj(h, g, a, w, tm):
    rows = h.shape[0]
    tile = pl.BlockSpec((tm, D_MODEL), lambda i: (i, 0))
    return pl.pallas_call(
        _oproj_kernel,
        grid=(rows // tm,),
        in_specs=[tile, _resident((2, D_MODEL)), tile, _resident((D_MODEL, D_MODEL))],
        out_specs=tile,
        out_shape=jax.ShapeDtypeStruct((rows, D_MODEL), F32),
        compiler_params=_params(1),
        name="attn_out_proj",
    )(h, g, a, w)


def _rope_tables(first_pos, n_pos):
    half = ROT_DIM // 2
    pos = jnp.arange(first_pos, first_pos + n_pos, dtype=F32)
    inv_freq = ROPE_THETA ** (-jnp.arange(0, ROT_DIM, 2, dtype=F32) / ROT_DIM)
    ang = pos[:, None] * inv_freq[None, :]
    cos, sin = jnp.cos(ang), jnp.sin(ang)
    ones = jnp.ones((n_pos, HEAD_DIM - ROT_DIM), F32)
    zeros = jnp.zeros((n_pos, HEAD_DIM - half), F32)
    cos64 = jnp.concatenate([cos, cos, ones], axis=1)
    lo64 = jnp.concatenate([-sin, zeros], axis=1)
    hi64 = jnp.concatenate([jnp.zeros((n_pos, half), F32), sin, ones * 0.0], axis=1)
    return tuple(jnp.concatenate([t, t], axis=1) for t in (cos64, lo64, hi64))


def _attn_mixer(hx, hm, g, wqk, wvt, lam_p, subln_g, wo, tm, tq, lambda_init):
    bsz, seq, _ = hx.shape
    qk_m, vt_m = _qkv_proj(hm, g, wqk, wvt, *_rope_tables(0, N_META), N_META)
    qk_m, vt_m = qk_m[0], vt_m[0]
    v_m = jnp.transpose(vt_m[:, :V_DIM, :], (2, 0, 1)).reshape(N_META, D_MODEL)
    a_m = _attention_small(qk_m, v_m, lam_p, subln_g, lambda_init)
    hm = _oproj(hm[0], g, a_m, wo, N_META)[None]
    qk, vt = _qkv_proj(hx, g, wqk, wvt, *_rope_tables(N_META, seq), tm)
    a = _attention_t(qk, vt, lam_p, subln_g, qk_m, vt_m, tq, HEADS_PER_STEP, lambda_init)
    hx = _oproj(hx.reshape(bsz * seq, D_MODEL), g, a.reshape(bsz * seq, D_MODEL), wo, tm)
    return hx.reshape(bsz, seq, D_MODEL), hm


ROW_TILE = 512
Q_TILE = 256
HEADS_PER_STEP = 4


def kernel(x, meta_tokens, ln_ffn1, ffn1_w_gu, ffn1_w_down, ln_mix, conv_w_in, conv_w, conv_w_out,
           attn_w_qkv, attn_lambda, attn_subln_g, attn_w_o, ln_ffn2, ffn2_w_gu, ffn2_w_down):
    bsz, seq, d = x.shape
    assert d == D_MODEL and seq % ROW_TILE == 0 and seq % Q_TILE == 0
    bf = lambda w: w.astype(BF16)
    ffn1_w_gu, ffn1_w_down, ffn2_w_gu, ffn2_w_down = map(bf, (ffn1_w_gu, ffn1_w_down, ffn2_w_gu, ffn2_w_down))
    conv_w_in, conv_w_out, attn_w_o = map(bf, (conv_w_in, conv_w_out, attn_w_o))
    attn_w_qk = bf(attn_w_qkv[:, :, :2 * D_MODEL])
    attn_w_vt = bf(jnp.swapaxes(attn_w_qkv[:, :, 2 * D_MODEL:], 1, 2))

    hx = x
    hm = meta_tokens.astype(x.dtype)[None]

    def ffn_both(hx, hm, g, wgu, wdn):
        hx = _ffn(hx.reshape(bsz * seq, D_MODEL), g, wgu, wdn, ROW_TILE).reshape(bsz, seq, D_MODEL)
        hm = _ffn(hm.reshape(N_META, D_MODEL), g, wgu, wdn, N_META).reshape(1, N_META, D_MODEL)
        return hx, hm

    for i in range(DEPTH):
        hx, hm = ffn_both(hx, hm, ln_ffn1[i], ffn1_w_gu[i], ffn1_w_down[i])
        j = i // N_MIXERS
        if i % N_MIXERS == 0:
            no_history = jnp.zeros((CARRY_ROWS, D_MODEL), F32)
            hm, meta_tail = _conv_mixer(hm, ln_mix[i], no_history, conv_w_in[j], conv_w[j], conv_w_out[j], N_META)
            hx, _ = _conv_mixer(hx, ln_mix[i], meta_tail[0], conv_w_in[j], conv_w[j], conv_w_out[j], ROW_TILE)
        else:
            lambda_init = 0.8 - 0.6 * math.exp(-0.3 * i)
            hx, hm = _attn_mixer(hx, hm, ln_mix[i], attn_w_qk[j], attn_w_vt[j], attn_lambda[j], attn_subln_g[j],
                                 attn_w_o[j], ROW_TILE, Q_TILE, lambda_init)
        hx, hm = ffn_both(hx, hm, ln_ffn2[i], ffn2_w_gu[i], ffn2_w_down[i])
    return hx
```

```python
import functools
import math

import jax
import jax.numpy as jnp
from jax import lax
from jax.experimental import pallas as pl
from jax.experimental.pallas import tpu as pltpu

D_MODEL = 1024
DEPTH = 4
N_MIXERS = 2
N_META = 16
D_FF = 2816
CONV_WIDTH = 3
HEAD_DIM = 64
V_DIM = 2 * HEAD_DIM
N_HEADS = D_MODEL // V_DIM
ROT_DIM = HEAD_DIM // 4
ROPE_THETA = 500000.0
NORM_EPS = 1e-6
SUBLN_EPS = 1e-5
NEG_INF = -1e30

F32 = jnp.float32
BF16 = jnp.bfloat16

V7X_VMEM_LIMIT_BYTES = 60 * 1024 * 1024
SUBLANES = 8
LANES = 128
CARRY_ROWS = SUBLANES


def _rms(t, g, eps):
    return t * lax.rsqrt(jnp.mean(t * t, axis=-1, keepdims=True) + eps) * g


def _resident(shape):
    zeros = (0,) * len(shape)
    return pl.BlockSpec(shape, lambda *_: zeros, pipeline_mode=pl.Buffered(1))


def _layer_block(shape, layer):
    zeros = (0,) * len(shape)
    return pl.BlockSpec((None,) + tuple(shape), lambda *_: (layer,) + zeros, pipeline_mode=pl.Buffered(1))


def _params(n_grid_dims):
    return pltpu.CompilerParams(
        dimension_semantics=("arbitrary",) * n_grid_dims,
        vmem_limit_bytes=V7X_VMEM_LIMIT_BYTES)


def _ffn_kernel(h_ref, g_ref, wgu_ref, wdn_ref, o_ref):
    h = h_ref[...]
    g = g_ref[...]
    hn = _rms(h, g[0:1], NORM_EPS).astype(BF16)
    gu = jnp.dot(hn, wgu_ref[...], preferred_element_type=F32)
    gate = gu[:, :D_FF]
    up = gu[:, D_FF:]
    act = (gate * jax.nn.sigmoid(gate) * up).astype(BF16)
    f = jnp.dot(act, wdn_ref[...], preferred_element_type=F32)
    o_ref[...] = h + 0.5 * _rms(f, g[1:2], NORM_EPS)


def _ffn(h, g, wgu, wdn, layer, tm):
    rows = h.shape[0]
    return pl.pallas_call(
        _ffn_kernel,
        grid=(rows // tm,),
        in_specs=[
            pl.BlockSpec((tm, D_MODEL), lambda i: (i, 0)),
            _layer_block((2, D_MODEL), layer),
            _layer_block((D_MODEL, 2 * D_FF), layer),
            _layer_block((D_FF, D_MODEL), layer),
        ],
        out_specs=pl.BlockSpec((tm, D_MODEL), lambda i: (i, 0)),
        out_shape=jax.ShapeDtypeStruct((rows, D_MODEL), F32),
        compiler_params=_params(1),
        name="ffn",
    )(h, g, wgu, wdn)


def _conv_kernel(h_ref, g_ref, tail_in_ref, win_ref, cw_ref, wout_ref,
                 o_ref, tail_out_ref, carry_ref, *, tm):
    @pl.when(pl.program_id(1) == 0)
    def _():
        carry_ref[...] = tail_in_ref[...]

    h = h_ref[0]
    g = g_ref[...]
    hn = _rms(h, g[0:1], NORM_EPS).astype(BF16)
    bcx = jnp.dot(hn, win_ref[...], preferred_element_type=F32)
    b = bcx[:, :D_MODEL]
    u = bcx[:, D_MODEL:2 * D_MODEL] * bcx[:, 2 * D_MODEL:]
    prev = carry_ref[...]
    row = lax.broadcasted_iota(jnp.int32, (tm, 1), 0)
    last = prev[CARRY_ROWS - 1:CARRY_ROWS]
    u1 = jnp.where(row == 0, last, pltpu.roll(u, 1, 0))
    u2 = jnp.where(row == 0, prev[CARRY_ROWS - 2:CARRY_ROWS - 1],
                   jnp.where(row == 1, last, pltpu.roll(u, 2, 0)))
    cw = cw_ref[...]
    conv = cw[0:1] * u2 + cw[1:2] * u1 + cw[2:3] * u
    tail = u[tm - CARRY_ROWS:]
    carry_ref[...] = tail
    tail_out_ref[0] = tail
    y = (b * conv).astype(BF16)
    m = jnp.dot(y, wout_ref[...], preferred_element_type=F32)
    o_ref[0] = h + _rms(m, g[1:2], NORM_EPS)


def _conv_mixer(h, g, tail_in, win, cw, wout, layer, mixer, tm):
    bsz, seq, _ = h.shape
    return pl.pallas_call(
        functools.partial(_conv_kernel, tm=tm),
        grid=(bsz, seq // tm),
        in_specs=[
            pl.BlockSpec((1, tm, D_MODEL), lambda b, j: (b, j, 0)),
            _layer_block((2, D_MODEL), layer),
            _resident((CARRY_ROWS, D_MODEL)),
            _layer_block((D_MODEL, 3 * D_MODEL), mixer),
            _layer_block((CONV_WIDTH, D_MODEL), mixer),
            _layer_block((D_MODEL, D_MODEL), mixer),
        ],
        out_specs=[
            pl.BlockSpec((1, tm, D_MODEL), lambda b, j: (b, j, 0)),
            pl.BlockSpec((1, CARRY_ROWS, D_MODEL), lambda b, j: (b, 0, 0)),
        ],
        out_shape=[
            jax.ShapeDtypeStruct((bsz, seq, D_MODEL), F32),
            jax.ShapeDtypeStruct((bsz, CARRY_ROWS, D_MODEL), F32),
        ],
        scratch_shapes=[pltpu.VMEM((CARRY_ROWS, D_MODEL), F32)],
        compiler_params=_params(2),
        name="conv_mixer",
    )(h, g, tail_in, win, cw, wout)


NT_DIMS = (((1,), (1,)), ((), ()))

ONES_ROWS = 16
VT_ROWS = V_DIM + ONES_ROWS
Q_SCALE = HEAD_DIM ** -0.5 * math.log2(math.e)


def _qkv_kernel(h_ref, g_ref, wqk_ref, wvt_ref, cos_ref, sin_lo_ref, sin_hi_ref, q2_ref, k_ref, vt_ref):
    g = g_ref[...]
    hn = _rms(h_ref[0], g[0:1], NORM_EPS).astype(BF16)
    qk = jnp.dot(hn, wqk_ref[...], preferred_element_type=F32)
    cos = cos_ref[...]
    sin_lo = sin_lo_ref[...]
    sin_hi = sin_hi_ref[...]
    half = ROT_DIM // 2
    first_component = lax.broadcasted_iota(jnp.int32, (1, LANES), 1) < HEAD_DIM
    for blk in range(2 * N_HEADS):
        x = qk[:, blk * LANES:(blk + 1) * LANES]
        r = x * cos + pltpu.roll(x, LANES - half, 1) * sin_lo + pltpu.roll(x, half, 1) * sin_hi
        if blk < N_HEADS:
            r = (r * Q_SCALE).astype(BF16)
            zero = jnp.zeros_like(r)
            q2_ref[0, 0, :, blk * LANES:(blk + 1) * LANES] = jnp.where(first_component, r, zero)
            q2_ref[0, 1, :, blk * LANES:(blk + 1) * LANES] = jnp.where(first_component, zero, r)
        else:
            k_ref[0, :, (blk - N_HEADS) * LANES:(blk - N_HEADS + 1) * LANES] = r.astype(BF16)
    vt = lax.dot_general(wvt_ref[...], hn, NT_DIMS, preferred_element_type=F32).astype(BF16)
    ones = jnp.ones((ONES_ROWS, vt.shape[1]), BF16)
    for head in range(N_HEADS):
        vt_ref[0, head, :V_DIM, :] = vt[head * V_DIM:(head + 1) * V_DIM]
        vt_ref[0, head, V_DIM:, :] = ones


def _qkv_proj(h, g, wqk, wvt, cos, sin_lo, sin_hi, layer, mixer, tm):
    bsz, seq, _ = h.shape
    table = pl.BlockSpec((tm, LANES), lambda b, j: (j, 0))
    return pl.pallas_call(
        _qkv_kernel,
        grid=(bsz, seq // tm),
        in_specs=[
            pl.BlockSpec((1, tm, D_MODEL), lambda b, j: (b, j, 0)),
            _layer_block((2, D_MODEL), layer),
            _layer_block((D_MODEL, 2 * D_MODEL), mixer),
            _layer_block((D_MODEL, D_MODEL), mixer),
            table, table, table,
        ],
        out_specs=[
            pl.BlockSpec((1, 2, tm, D_MODEL), lambda b, j: (b, 0, j, 0)),
            pl.BlockSpec((1, tm, D_MODEL), lambda b, j: (b, j, 0)),
            pl.BlockSpec((1, N_HEADS, VT_ROWS, tm), lambda b, j: (b, 0, 0, j)),
        ],
        out_shape=[
            jax.ShapeDtypeStruct((bsz, 2, seq, D_MODEL), BF16),
            jax.ShapeDtypeStruct((bsz, seq, D_MODEL), BF16),
            jax.ShapeDtypeStruct((bsz, N_HEADS, VT_ROWS, seq), BF16),
        ],
        compiler_params=_params(2),
        name="qkv_proj",
    )(h, g, wqk, wvt, cos, sin_lo, sin_hi)


SCORE_LOOKAHEAD = 2


def _attn_t_kernel(lam_ref, sg_ref, q2_ref, k_ref, vt_ref, kp_ref, vtp_ref, o_ref, acc_ref, st_ref,
                   *, tq, n_heads, lambda_init):
    seq = k_ref.shape[1]
    ahead = min(SCORE_LOOKAHEAD, n_heads)
    key_i = lax.broadcasted_iota(jnp.int32, (tq, 2 * tq), 0)
    qry_i = lax.broadcasted_iota(jnp.int32, (tq, 2 * tq), 1)
    causal = key_i <= jnp.where(qry_i >= tq, qry_i - tq, qry_i)
    lam_p = lam_ref[...]
    lam = (jnp.exp(jnp.sum(lam_p[0:1] * lam_p[1:2], axis=1, keepdims=True))
           - jnp.exp(jnp.sum(lam_p[2:3] * lam_p[3:4], axis=1, keepdims=True)) + lambda_init)
    sg = sg_ref[...]
    heads = [slice(g * LANES, (g + 1) * LANES) for g in range(n_heads)]

    def q_tile(qi, _):
        q_start = pl.multiple_of(qi * tq, tq)
        qqs = [jnp.concatenate([q2_ref[0, 0, pl.ds(q_start, tq), hs],
                                q2_ref[0, 1, pl.ds(q_start, tq), hs]], axis=0) for hs in heads]

        def scores(g, start):
            return lax.dot_general(k_ref[0, pl.ds(start, tq), heads[g]], qqs[g], NT_DIMS,
                                   preferred_element_type=F32)

        def first_scores(g):
            prefix = lax.dot_general(kp_ref[:, heads[g]], qqs[g], NT_DIMS, preferred_element_type=F32)
            return prefix, scores(g, q_start)

        def fold(start, next_start, maxes):
            first = maxes is None
            pending = [first_scores(g) for g in range(ahead)] if first else [None] * ahead
            new_maxes = []
            for g in range(n_heads):
                st = pending.pop(0)
                if st is None:
                    st = st_ref[g]
                upcoming = g + ahead
                if upcoming < n_heads:
                    pending.append(first_scores(upcoming) if first else scores(upcoming, start))
                else:
                    st_ref[upcoming - n_heads] = scores(upcoming - n_heads, next_start)
                vt = vt_ref[0, g, :, pl.ds(start, tq)]
                if first:
                    st_p, st = st
                    st = jnp.where(causal, st, NEG_INF)
                    m_new = jnp.maximum(jnp.max(st, axis=0, keepdims=True), jnp.max(st_p, axis=0, keepdims=True))
                    acc_ref[g] = (
                        jnp.dot(vtp_ref[g], jnp.exp2(st_p - m_new).astype(BF16), preferred_element_type=F32)
                        + jnp.dot(vt, jnp.exp2(st - m_new).astype(BF16), preferred_element_type=F32))
                else:
                    m_new = jnp.maximum(maxes[g], jnp.max(st, axis=0, keepdims=True))
                    alpha = jnp.exp2(maxes[g] - m_new)
                    pv = jnp.dot(vt, jnp.exp2(st - m_new).astype(BF16), preferred_element_type=F32)
                    acc_ref[g] = alpha * acc_ref[g] + pv
                new_maxes.append(m_new)
            return tuple(new_maxes)

        def block(kb, maxes):
            start = pl.multiple_of(kb * tq, tq)
            return fold(start, start + tq, maxes)

        lax.fori_loop(0, qi, block, fold(q_start, 0, None))

        for g, hs in enumerate(heads):
            acc = acc_ref[g]
            pv = acc[:V_DIM] * (1.0 / acc[V_DIM:V_DIM + 1])
            o = pv[:, :tq] - lam * pv[:, tq:]
            o = o * lax.rsqrt(jnp.mean(o * o, axis=0, keepdims=True) + SUBLN_EPS) * sg * (1.0 - lambda_init)
            o_ref[0, pl.ds(q_start, tq), hs] = o.T.astype(BF16)
        return 0

    lax.fori_loop(0, seq // tq, q_tile, 0)


def _attention_t(q2, k, vt, lam_p, subln_g, prefix_k, prefix_vt, mixer, tq, n_heads, lambda_init):
    bsz, seq, _ = k.shape
    n_prefix = prefix_k.shape[0]
    groups = N_HEADS // n_heads
    width = n_heads * LANES
    return pl.pallas_call(
        functools.partial(_attn_t_kernel, tq=tq, n_heads=n_heads, lambda_init=lambda_init),
        grid=(bsz, groups),
        in_specs=[
            _layer_block((4, HEAD_DIM), mixer),
            _layer_block((V_DIM, 1), mixer),
            pl.BlockSpec((1, 2, seq, width), lambda b, h: (b, 0, 0, h)),
            pl.BlockSpec((1, seq, width), lambda b, h: (b, 0, h)),
            pl.BlockSpec((1, n_heads, VT_ROWS, seq), lambda b, h: (b, h, 0, 0)),
            pl.BlockSpec((n_prefix, width), lambda b, h: (0, h)),
            pl.BlockSpec((n_heads, VT_ROWS, n_prefix), lambda b, h: (h, 0, 0)),
        ],
        out_specs=pl.BlockSpec((1, seq, width), lambda b, h: (b, 0, h)),
        out_shape=jax.ShapeDtypeStruct((bsz, seq, D_MODEL), BF16),
        scratch_shapes=[pltpu.VMEM((n_heads, VT_ROWS, 2 * tq), F32),
                        pltpu.VMEM((min(SCORE_LOOKAHEAD, n_heads), tq, 2 * tq), F32)],
        compiler_params=_params(2),
        name="diff_attention",
    )(lam_p, subln_g.reshape(-1, V_DIM, 1), q2, k, vt, prefix_k, prefix_vt)


def _attn_small_kernel(lam_ref, sg_ref, q2_ref, k_ref, v_ref, o_ref, *, lambda_init):
    tq = k_ref.shape[0]
    qq = jnp.concatenate([q2_ref[0], q2_ref[1]], axis=0)
    s = lax.dot_general(qq, k_ref[...], NT_DIMS, preferred_element_type=F32)
    r = lax.broadcasted_iota(jnp.int32, (2 * tq, tq), 0)
    c = lax.broadcasted_iota(jnp.int32, (2 * tq, tq), 1)
    s = jnp.where(c <= jnp.where(r >= tq, r - tq, r), s, NEG_INF)
    p = jnp.exp2(s - jnp.max(s, axis=1, keepdims=True))
    l = jnp.sum(p, axis=1, keepdims=True)
    pv = jnp.dot(p.astype(BF16), v_ref[...], preferred_element_type=F32) / l
    lam_p = lam_ref[...]
    lam = (jnp.exp(jnp.sum(lam_p[0:1] * lam_p[1:2], axis=1, keepdims=True))
           - jnp.exp(jnp.sum(lam_p[2:3] * lam_p[3:4], axis=1, keepdims=True)) + lambda_init)
    o = pv[:tq] - lam * pv[tq:]
    o_ref[...] = (_rms(o, sg_ref[...], SUBLN_EPS) * (1.0 - lambda_init)).astype(BF16)


def _attention_small(q2, k, v, lam_p, subln_g, mixer, lambda_init):
    seq = k.shape[0]
    return pl.pallas_call(
        functools.partial(_attn_small_kernel, lambda_init=lambda_init),
        grid=(N_HEADS,),
        in_specs=[
            _layer_block((4, HEAD_DIM), mixer),
            _layer_block((1, V_DIM), mixer),
            pl.BlockSpec((2, seq, LANES), lambda h: (0, 0, h)),
            pl.BlockSpec((seq, LANES), lambda h: (0, h)),
            pl.BlockSpec((seq, LANES), lambda h: (0, h)),
        ],
        out_specs=pl.BlockSpec((seq, LANES), lambda h: (0, h)),
        out_shape=jax.ShapeDtypeStruct((seq, D_MODEL), BF16),
        compiler_params=_params(1),
        name="diff_attention_meta",
    )(lam_p, subln_g.reshape(-1, 1, V_DIM), q2, k, v)


def _oproj_kernel(h_ref, g_ref, a_ref, w_ref, o_ref):
    m = jnp.dot(a_ref[...], w_ref[...], preferred_element_type=F32)
    o_ref[...] = h_ref[...] + _rms(m, g_ref[...][1:2], NORM_EPS)


def _oproj(h, g, a, w, layer, mixer, tm):
    rows = h.shape[0]
    tile = pl.BlockSpec((tm, D_MODEL), lambda i: (i, 0))
    return pl.pallas_call(
        _oproj_kernel,
        grid=(rows // tm,),
        in_specs=[tile, _layer_block((2, D_MODEL), layer), tile, _layer_block((D_MODEL, D_MODEL), mixer)],
        out_specs=tile,
        out_shape=jax.ShapeDtypeStruct((rows, D_MODEL), F32),
        compiler_params=_params(1),
        name="attn_out_proj",
    )(h, g, a, w)


def _rope_tables(first_pos, n_pos):
    half = ROT_DIM // 2
    pos = jnp.arange(first_pos, first_pos + n_pos, dtype=F32)
    inv_freq = ROPE_THETA ** (-jnp.arange(0, ROT_DIM, 2, dtype=F32) / ROT_DIM)
    ang = pos[:, None] * inv_freq[None, :]
    cos, sin = jnp.cos(ang), jnp.sin(ang)
    ones = jnp.ones((n_pos, HEAD_DIM - ROT_DIM), F32)
    zeros = jnp.zeros((n_pos, HEAD_DIM - half), F32)
    cos64 = jnp.concatenate([cos, cos, ones], axis=1)
    lo64 = jnp.concatenate([-sin, zeros], axis=1)
    hi64 = jnp.concatenate([jnp.zeros((n_pos, half), F32), sin, ones * 0.0], axis=1)
    return tuple(jnp.concatenate([t, t], axis=1) for t in (cos64, lo64, hi64))


def _attn_mixer(hx, hm, g, wqk, wvt, lam_p, subln_g, wo, layer, mixer, tm, tq, lambda_init):
    bsz, seq, _ = hx.shape
    q2_m, k_m, vt_m = (t[0] for t in _qkv_proj(hm, g, wqk, wvt, *_rope_tables(0, N_META), layer, mixer, N_META))
    v_m = jnp.transpose(vt_m[:, :V_DIM, :], (2, 0, 1)).reshape(N_META, D_MODEL)
    a_m = _attention_small(q2_m, k_m, v_m, lam_p, subln_g, mixer, lambda_init)
    hm = _oproj(hm[0], g, a_m, wo, layer, mixer, N_META)[None]
    q2, k, vt = _qkv_proj(hx, g, wqk, wvt, *_rope_tables(N_META, seq), layer, mixer, tm)
    a = _attention_t(q2, k, vt, lam_p, subln_g, k_m, vt_m, mixer, tq, HEADS_PER_STEP, lambda_init)
    hx = _oproj(hx.reshape(bsz * seq, D_MODEL), g, a.reshape(bsz * seq, D_MODEL), wo, layer, mixer, tm)
    return hx.reshape(bsz, seq, D_MODEL), hm


ROW_TILE = 512
Q_TILE = 256
HEADS_PER_STEP = 8


def kernel(x, meta_tokens, ln_ffn1, ffn1_w_gu, ffn1_w_down, ln_mix, conv_w_in, conv_w, conv_w_out,
           attn_w_qkv, attn_lambda, attn_subln_g, attn_w_o, ln_ffn2, ffn2_w_gu, ffn2_w_down):
    bsz, seq, d = x.shape
    assert d == D_MODEL and seq % ROW_TILE == 0 and seq % Q_TILE == 0
    bf = lambda w: w.astype(BF16)
    ffn1_w_gu, ffn1_w_down, ffn2_w_gu, ffn2_w_down = map(bf, (ffn1_w_gu, ffn1_w_down, ffn2_w_gu, ffn2_w_down))
    conv_w_in, conv_w_out, attn_w_o = map(bf, (conv_w_in, conv_w_out, attn_w_o))
    attn_w_qk = bf(attn_w_qkv[:, :, :2 * D_MODEL])
    attn_w_vt = bf(jnp.swapaxes(attn_w_qkv[:, :, 2 * D_MODEL:], 1, 2))

    hx = x
    hm = meta_tokens.astype(x.dtype)[None]

    def ffn_both(hx, hm, g, wgu, wdn, layer):
        hx = _ffn(hx.reshape(bsz * seq, D_MODEL), g, wgu, wdn, layer, ROW_TILE).reshape(bsz, seq, D_MODEL)
        hm = _ffn(hm.reshape(N_META, D_MODEL), g, wgu, wdn, layer, N_META).reshape(1, N_META, D_MODEL)
        return hx, hm

    no_history = jnp.zeros((CARRY_ROWS, D_MODEL), F32)
    for i in range(DEPTH):
        hx, hm = ffn_both(hx, hm, ln_ffn1, ffn1_w_gu, ffn1_w_down, i)
        j = i // N_MIXERS
        if i % N_MIXERS == 0:
            hm, meta_tail = _conv_mixer(hm, ln_mix, no_history, conv_w_in, conv_w, conv_w_out, i, j, N_META)
            hx, _ = _conv_mixer(hx, ln_mix, meta_tail[0], conv_w_in, conv_w, conv_w_out, i, j, ROW_TILE)
        else:
            lambda_init = 0.8 - 0.6 * math.exp(-0.3 * i)
            hx, hm = _attn_mixer(hx, hm, ln_mix, attn_w_qk, attn_w_vt, attn_lambda, attn_subln_g, attn_w_o,
                                 i, j, ROW_TILE, Q_TILE, lambda_init)
        hx, hm = ffn_both(hx, hm, ln_ffn2, ffn2_w_gu, ffn2_w_down, i)
    return hx
```

```python
import functools
import math

import jax
import jax.numpy as jnp
from jax import lax
from jax.experimental import pallas as pl
from jax.experimental.pallas import tpu as pltpu

D_MODEL = 1024
DEPTH = 4
N_MIXERS = 2
N_META = 16
D_FF = 2816
CONV_WIDTH = 3
HEAD_DIM = 64
V_DIM = 2 * HEAD_DIM
N_HEADS = D_MODEL // V_DIM
ROT_DIM = HEAD_DIM // 4
ROPE_THETA = 500000.0
NORM_EPS = 1e-6
SUBLN_EPS = 1e-5
NEG_INF = -1e30

F32 = jnp.float32
BF16 = jnp.bfloat16

V7X_VMEM_LIMIT_BYTES = 60 * 1024 * 1024
SUBLANES = 8
LANES = 128
CARRY_ROWS = SUBLANES


def _rms(t, g, eps):
    return t * lax.rsqrt(jnp.mean(t * t, axis=-1, keepdims=True) + eps) * g


def _resident(shape):
    zeros = (0,) * len(shape)
    return pl.BlockSpec(shape, lambda *_: zeros, pipeline_mode=pl.Buffered(1))


def _layer_block(shape, layer):
    zeros = (0,) * len(shape)
    return pl.BlockSpec((None,) + tuple(shape), lambda *_: (layer,) + zeros, pipeline_mode=pl.Buffered(1))


def _params(n_grid_dims):
    return pltpu.CompilerParams(
        dimension_semantics=("arbitrary",) * n_grid_dims,
        vmem_limit_bytes=V7X_VMEM_LIMIT_BYTES)


FFN_SUB_TILES = 4
BF16_ROWS = 16


def _ffn_kernel(*refs, n_casts):
    hx_ref, hm_ref, g_ref, wgu_ref, wdn_ref = refs[:5]
    w32_refs = refs[5:5 + n_casts]
    ox_ref, om_ref = refs[5 + n_casts:7 + n_casts]
    w16_refs = refs[7 + n_casts:]
    for w32_ref, w16_ref in zip(w32_refs, w16_refs):
        w16_ref[...] = w32_ref[...].astype(BF16)
    g = g_ref[...]
    sub = hx_ref.shape[0] // FFN_SUB_TILES

    def ffn(tiles):
        hns = [_rms(h, g[0:1], NORM_EPS).astype(BF16) for h in tiles]
        gus = [jnp.dot(hn, wgu_ref[...], preferred_element_type=F32) for hn in hns]
        acts = [(gu[:, :D_FF] * jax.nn.sigmoid(gu[:, :D_FF]) * gu[:, D_FF:]).astype(BF16) for gu in gus]
        fs = [jnp.dot(act, wdn_ref[...], preferred_element_type=F32) for act in acts]
        return [h + 0.5 * _rms(f, g[1:2], NORM_EPS) for h, f in zip(tiles, fs)]

    def real_tiles():
        return [hx_ref[s * sub:(s + 1) * sub] for s in range(FFN_SUB_TILES)]

    def store_real(outs):
        for s, o in enumerate(outs):
            ox_ref[s * sub:(s + 1) * sub] = o

    @pl.when(pl.program_id(0) == 0)
    def _():
        tiles = real_tiles()
        tiles[0] = jnp.concatenate([hm_ref[...], tiles[0]], axis=0)
        outs = ffn(tiles)
        om_ref[...] = outs[0][:N_META]
        outs[0] = outs[0][N_META:]
        store_real(outs)

    @pl.when(pl.program_id(0) > 0)
    def _():
        store_real(ffn(real_tiles()))


def _slab_rows(n_rows, steps):
    for rows in range(BF16_ROWS, n_rows + 1, BF16_ROWS):
        if n_rows % rows == 0 and n_rows // rows <= steps:
            return rows
    raise ValueError((n_rows, steps))


def _ffn(hx, hm, g, wgu, wdn, layer, tm, casts=()):
    rows = hx.shape[0]
    steps = rows // tm
    tile = pl.BlockSpec((tm, D_MODEL), lambda i: (i, 0))
    in_specs = [tile, _resident((N_META, D_MODEL)), _layer_block((2, D_MODEL), layer),
                _resident((D_MODEL, 2 * D_FF)), _resident((D_FF, D_MODEL))]
    out_specs = [tile, pl.BlockSpec((N_META, D_MODEL), lambda i: (0, 0))]
    out_shape = [jax.ShapeDtypeStruct((rows, D_MODEL), F32), jax.ShapeDtypeStruct((N_META, D_MODEL), F32)]
    for w32, index in casts:
        _, n_rows, n_cols = w32.shape
        slab = _slab_rows(n_rows, steps)
        last = n_rows // slab - 1
        in_specs.append(pl.BlockSpec((None, slab, n_cols),
                                     lambda i, index=index, last=last: (index, jnp.minimum(i, last), 0)))
        out_specs.append(pl.BlockSpec((slab, n_cols), lambda i, last=last: (jnp.minimum(i, last), 0)))
        out_shape.append(jax.ShapeDtypeStruct((n_rows, n_cols), BF16))
    return pl.pallas_call(
        functools.partial(_ffn_kernel, n_casts=len(casts)),
        grid=(steps,),
        in_specs=in_specs,
        out_specs=out_specs,
        out_shape=out_shape,
        compiler_params=_params(1),
        name="ffn",
    )(hx, hm, g, wgu, wdn, *[c[0] for c in casts])


def _conv_kernel(h_ref, g_ref, tail_in_ref, win_ref, cw_ref, wout_ref,
                 o_ref, tail_out_ref, carry_ref, *, tm):
    @pl.when(pl.program_id(1) == 0)
    def _():
        carry_ref[...] = tail_in_ref[...]

    h = h_ref[0]
    g = g_ref[...]
    hn = _rms(h, g[0:1], NORM_EPS).astype(BF16)
    bcx = jnp.dot(hn, win_ref[...], preferred_element_type=F32)
    b = bcx[:, :D_MODEL]
    u = bcx[:, D_MODEL:2 * D_MODEL] * bcx[:, 2 * D_MODEL:]
    prev = carry_ref[...]
    row = lax.broadcasted_iota(jnp.int32, (tm, 1), 0)
    last = prev[CARRY_ROWS - 1:CARRY_ROWS]
    u1 = jnp.where(row == 0, last, pltpu.roll(u, 1, 0))
    u2 = jnp.where(row == 0, prev[CARRY_ROWS - 2:CARRY_ROWS - 1],
                   jnp.where(row == 1, last, pltpu.roll(u, 2, 0)))
    cw = cw_ref[...]
    conv = cw[0:1] * u2 + cw[1:2] * u1 + cw[2:3] * u
    tail = u[tm - CARRY_ROWS:]
    carry_ref[...] = tail
    tail_out_ref[0] = tail
    y = (b * conv).astype(BF16)
    m = jnp.dot(y, wout_ref[...], preferred_element_type=F32)
    o_ref[0] = h + _rms(m, g[1:2], NORM_EPS)


def _conv_mixer(h, g, tail_in, win, cw, wout, layer, mixer, tm):
    bsz, seq, _ = h.shape
    return pl.pallas_call(
        functools.partial(_conv_kernel, tm=tm),
        grid=(bsz, seq // tm),
        in_specs=[
            pl.BlockSpec((1, tm, D_MODEL), lambda b, j: (b, j, 0)),
            _layer_block((2, D_MODEL), layer),
            _resident((CARRY_ROWS, D_MODEL)),
            _resident((D_MODEL, 3 * D_MODEL)),
            _layer_block((CONV_WIDTH, D_MODEL), mixer),
            _resident((D_MODEL, D_MODEL)),
        ],
        out_specs=[
            pl.BlockSpec((1, tm, D_MODEL), lambda b, j: (b, j, 0)),
            pl.BlockSpec((1, CARRY_ROWS, D_MODEL), lambda b, j: (b, 0, 0)),
        ],
        out_shape=[
            jax.ShapeDtypeStruct((bsz, seq, D_MODEL), F32),
            jax.ShapeDtypeStruct((bsz, CARRY_ROWS, D_MODEL), F32),
        ],
        scratch_shapes=[pltpu.VMEM((CARRY_ROWS, D_MODEL), F32)],
        compiler_params=_params(2),
        name="conv_mixer",
    )(h, g, tail_in, win, cw, wout)


NT_DIMS = (((1,), (1,)), ((), ()))

ONES_ROWS = 16
VT_ROWS = V_DIM + ONES_ROWS
Q_SCALE = HEAD_DIM ** -0.5 * math.log2(math.e)


def _qkv_kernel(h_ref, g_ref, w_ref, cos_ref, sin_lo_ref, sin_hi_ref, q2_ref, k_ref, vt_ref):
    g = g_ref[...]
    hn = _rms(h_ref[0], g[0:1], NORM_EPS).astype(BF16)
    qk = jnp.dot(hn, w_ref[...], preferred_element_type=F32)
    cos = cos_ref[...]
    sin_lo = sin_lo_ref[...]
    sin_hi = sin_hi_ref[...]
    half = ROT_DIM // 2
    first_component = lax.broadcasted_iota(jnp.int32, (1, LANES), 1) < HEAD_DIM
    for blk in range(2 * N_HEADS):
        x = qk[:, blk * LANES:(blk + 1) * LANES]
        r = x * cos + pltpu.roll(x, LANES - half, 1) * sin_lo + pltpu.roll(x, half, 1) * sin_hi
        if blk < N_HEADS:
            r = (r * Q_SCALE).astype(BF16)
            zero = jnp.zeros_like(r)
            q2_ref[0, 0, :, blk * LANES:(blk + 1) * LANES] = jnp.where(first_component, r, zero)
            q2_ref[0, 1, :, blk * LANES:(blk + 1) * LANES] = jnp.where(first_component, zero, r)
        else:
            k_ref[0, :, (blk - N_HEADS) * LANES:(blk - N_HEADS + 1) * LANES] = r.astype(BF16)
    ones = jnp.ones((ONES_ROWS, qk.shape[0]), BF16)
    for head in range(N_HEADS):
        v = qk[:, 2 * D_MODEL + head * V_DIM:2 * D_MODEL + (head + 1) * V_DIM]
        vt_ref[0, head, :V_DIM, :] = v.T.astype(BF16)
        vt_ref[0, head, V_DIM:, :] = ones


def _qkv_proj(h, g, w, cos, sin_lo, sin_hi, layer, tm):
    bsz, seq, _ = h.shape
    table = pl.BlockSpec((tm, LANES), lambda b, j: (j, 0))
    return pl.pallas_call(
        _qkv_kernel,
        grid=(bsz, seq // tm),
        in_specs=[
            pl.BlockSpec((1, tm, D_MODEL), lambda b, j: (b, j, 0)),
            _layer_block((2, D_MODEL), layer),
            _resident((D_MODEL, 3 * D_MODEL)),
            table, table, table,
        ],
        out_specs=[
            pl.BlockSpec((1, 2, tm, D_MODEL), lambda b, j: (b, 0, j, 0)),
            pl.BlockSpec((1, tm, D_MODEL), lambda b, j: (b, j, 0)),
            pl.BlockSpec((1, N_HEADS, VT_ROWS, tm), lambda b, j: (b, 0, 0, j)),
        ],
        out_shape=[
            jax.ShapeDtypeStruct((bsz, 2, seq, D_MODEL), BF16),
            jax.ShapeDtypeStruct((bsz, seq, D_MODEL), BF16),
            jax.ShapeDtypeStruct((bsz, N_HEADS, VT_ROWS, seq), BF16),
        ],
        compiler_params=_params(2),
        name="qkv_proj",
    )(h, g, w, cos, sin_lo, sin_hi)


SCORE_LOOKAHEAD = 2


def _attn_t_kernel(lam_ref, sg_ref, q2_ref, k_ref, vt_ref, kp_ref, vtp_ref, o_ref, acc_ref, st_ref,
                   *, tq, n_heads, lambda_init):
    seq = k_ref.shape[1]
    ahead = min(SCORE_LOOKAHEAD, n_heads)
    key_i = lax.broadcasted_iota(jnp.int32, (tq, 2 * tq), 0)
    qry_i = lax.broadcasted_iota(jnp.int32, (tq, 2 * tq), 1)
    causal = key_i <= jnp.where(qry_i >= tq, qry_i - tq, qry_i)
    lam_p = lam_ref[...]
    lam = (jnp.exp(jnp.sum(lam_p[0:1] * lam_p[1:2], axis=1, keepdims=True))
           - jnp.exp(jnp.sum(lam_p[2:3] * lam_p[3:4], axis=1, keepdims=True)) + lambda_init)
    sg = sg_ref[...]
    heads = [slice(g * LANES, (g + 1) * LANES) for g in range(n_heads)]

    def q_tile(qi, _):
        q_start = pl.multiple_of(qi * tq, tq)
        qqs = [jnp.concatenate([q2_ref[0, 0, pl.ds(q_start, tq), hs],
                                q2_ref[0, 1, pl.ds(q_start, tq), hs]], axis=0) for hs in heads]

        def scores(g, start):
            return lax.dot_general(k_ref[0, pl.ds(start, tq), heads[g]], qqs[g], NT_DIMS,
                                   preferred_element_type=F32)

        def first_scores(g):
            prefix = lax.dot_general(kp_ref[:, heads[g]], qqs[g], NT_DIMS, preferred_element_type=F32)
            return prefix, scores(g, q_start)

        def fold(start, next_start, maxes):
            first = maxes is None
            pending = [first_scores(g) for g in range(ahead)] if first else [None] * ahead
            new_maxes = []
            for g in range(n_heads):
                st = pending.pop(0)
                if st is None:
                    st = st_ref[g]
                upcoming = g + ahead
                if upcoming < n_heads:
                    pending.append(first_scores(upcoming) if first else scores(upcoming, start))
                else:
                    st_ref[upcoming - n_heads] = scores(upcoming - n_heads, next_start)
                vt = vt_ref[0, g, :, pl.ds(start, tq)]
                if first:
                    st_p, st = st
                    st = jnp.where(causal, st, NEG_INF)
                    m_new = jnp.maximum(jnp.max(st, axis=0, keepdims=True), jnp.max(st_p, axis=0, keepdims=True))
                    acc_ref[g] = (
                        jnp.dot(vtp_ref[g], jnp.exp2(st_p - m_new).astype(BF16), preferred_element_type=F32)
                        + jnp.dot(vt, jnp.exp2(st - m_new).astype(BF16), preferred_element_type=F32))
                else:
                    m_new = jnp.maximum(maxes[g], jnp.max(st, axis=0, keepdims=True))
                    alpha = jnp.exp2(maxes[g] - m_new)
                    pv = jnp.dot(vt, jnp.exp2(st - m_new).astype(BF16), preferred_element_type=F32)
                    acc_ref[g] = alpha * acc_ref[g] + pv
                new_maxes.append(m_new)
            return tuple(new_maxes)

        def block(kb, maxes):
            start = pl.multiple_of(kb * tq, tq)
            return fold(start, start + tq, maxes)

        lax.fori_loop(0, qi, block, fold(q_start, 0, None))

        for g, hs in enumerate(heads):
            acc = acc_ref[g]
            pv = acc[:V_DIM] * (1.0 / acc[V_DIM:V_DIM + 1])
            o = pv[:, :tq] - lam * pv[:, tq:]
            o = o * lax.rsqrt(jnp.mean(o * o, axis=0, keepdims=True) + SUBLN_EPS) * sg * (1.0 - lambda_init)
            o_ref[0, pl.ds(q_start, tq), hs] = o.T.astype(BF16)
        return 0

    lax.fori_loop(0, seq // tq, q_tile, 0)


def _attention_t(q2, k, vt, lam_p, subln_g, prefix_k, prefix_vt, mixer, tq, n_heads, lambda_init):
    bsz, seq, _ = k.shape
    n_prefix = prefix_k.shape[0]
    groups = N_HEADS // n_heads
    width = n_heads * LANES
    return pl.pallas_call(
        functools.partial(_attn_t_kernel, tq=tq, n_heads=n_heads, lambda_init=lambda_init),
        grid=(bsz, groups),
        in_specs=[
            _layer_block((4, HEAD_DIM), mixer),
            _layer_block((V_DIM, 1), mixer),
            pl.BlockSpec((1, 2, seq, width), lambda b, h: (b, 0, 0, h)),
            pl.BlockSpec((1, seq, width), lambda b, h: (b, 0, h)),
            pl.BlockSpec((1, n_heads, VT_ROWS, seq), lambda b, h: (b, h, 0, 0)),
            pl.BlockSpec((n_prefix, width), lambda b, h: (0, h)),
            pl.BlockSpec((n_heads, VT_ROWS, n_prefix), lambda b, h: (h, 0, 0)),
        ],
        out_specs=pl.BlockSpec((1, seq, width), lambda b, h: (b, 0, h)),
        out_shape=jax.ShapeDtypeStruct((bsz, seq, D_MODEL), BF16),
        scratch_shapes=[pltpu.VMEM((n_heads, VT_ROWS, 2 * tq), F32),
                        pltpu.VMEM((min(SCORE_LOOKAHEAD, n_heads), tq, 2 * tq), F32)],
        compiler_params=_params(2),
        name="diff_attention",
    )(lam_p, subln_g.reshape(-1, V_DIM, 1), q2, k, vt, prefix_k, prefix_vt)


def _attn_small_kernel(lam_ref, sg_ref, q2_ref, k_ref, v_ref, o_ref, *, lambda_init):
    tq = k_ref.shape[0]
    qq = jnp.concatenate([q2_ref[0], q2_ref[1]], axis=0)
    s = lax.dot_general(qq, k_ref[...], NT_DIMS, preferred_element_type=F32)
    r = lax.broadcasted_iota(jnp.int32, (2 * tq, tq), 0)
    c = lax.broadcasted_iota(jnp.int32, (2 * tq, tq), 1)
    s = jnp.where(c <= jnp.where(r >= tq, r - tq, r), s, NEG_INF)
    p = jnp.exp2(s - jnp.max(s, axis=1, keepdims=True))
    l = jnp.sum(p, axis=1, keepdims=True)
    pv = jnp.dot(p.astype(BF16), v_ref[...], preferred_element_type=F32) / l
    lam_p = lam_ref[...]
    lam = (jnp.exp(jnp.sum(lam_p[0:1] * lam_p[1:2], axis=1, keepdims=True))
           - jnp.exp(jnp.sum(lam_p[2:3] * lam_p[3:4], axis=1, keepdims=True)) + lambda_init)
    o = pv[:tq] - lam * pv[tq:]
    o_ref[...] = (_rms(o, sg_ref[...], SUBLN_EPS) * (1.0 - lambda_init)).astype(BF16)


def _attention_small(q2, k, v, lam_p, subln_g, mixer, lambda_init):
    seq = k.shape[0]
    return pl.pallas_call(
        functools.partial(_attn_small_kernel, lambda_init=lambda_init),
        grid=(N_HEADS,),
        in_specs=[
            _layer_block((4, HEAD_DIM), mixer),
            _layer_block((1, V_DIM), mixer),
            pl.BlockSpec((2, seq, LANES), lambda h: (0, 0, h)),
            pl.BlockSpec((seq, LANES), lambda h: (0, h)),
            pl.BlockSpec((seq, LANES), lambda h: (0, h)),
        ],
        out_specs=pl.BlockSpec((seq, LANES), lambda h: (0, h)),
        out_shape=jax.ShapeDtypeStruct((seq, D_MODEL), BF16),
        compiler_params=_params(1),
        name="diff_attention_meta",
    )(lam_p, subln_g.reshape(-1, 1, V_DIM), q2, k, v)


def _oproj_kernel(h_ref, g_ref, a_ref, w_ref, o_ref):
    m = jnp.dot(a_ref[...], w_ref[...], preferred_element_type=F32)
    o_ref[...] = h_ref[...] + _rms(m, g_ref[...][1:2], NORM_EPS)


def _oproj(h, g, a, w, layer, tm):
    rows = h.shape[0]
    tile = pl.BlockSpec((tm, D_MODEL), lambda i: (i, 0))
    return pl.pallas_call(
        _oproj_kernel,
        grid=(rows // tm,),
        in_specs=[tile, _layer_block((2, D_MODEL), layer), tile, _resident((D_MODEL, D_MODEL))],
        out_specs=tile,
        out_shape=jax.ShapeDtypeStruct((rows, D_MODEL), F32),
        compiler_params=_params(1),
        name="attn_out_proj",
    )(h, g, a, w)


def _rope_tables(first_pos, n_pos):
    half = ROT_DIM // 2
    pos = jnp.arange(first_pos, first_pos + n_pos, dtype=F32)
    inv_freq = ROPE_THETA ** (-jnp.arange(0, ROT_DIM, 2, dtype=F32) / ROT_DIM)
    ang = pos[:, None] * inv_freq[None, :]
    cos, sin = jnp.cos(ang), jnp.sin(ang)
    ones = jnp.ones((n_pos, HEAD_DIM - ROT_DIM), F32)
    zeros = jnp.zeros((n_pos, HEAD_DIM - half), F32)
    cos64 = jnp.concatenate([cos, cos, ones], axis=1)
    lo64 = jnp.concatenate([-sin, zeros], axis=1)
    hi64 = jnp.concatenate([jnp.zeros((n_pos, half), F32), sin, ones * 0.0], axis=1)
    return tuple(jnp.concatenate([t, t], axis=1) for t in (cos64, lo64, hi64))


def _attn_mixer(hx, hm, g, wqkv, lam_p, subln_g, wo, layer, mixer, tm, tq, lambda_init):
    bsz, seq, _ = hx.shape
    q2_m, k_m, vt_m = (t[0] for t in _qkv_proj(hm, g, wqkv, *_rope_tables(0, N_META), layer, N_META))
    v_m = jnp.transpose(vt_m[:, :V_DIM, :], (2, 0, 1)).reshape(N_META, D_MODEL)
    a_m = _attention_small(q2_m, k_m, v_m, lam_p, subln_g, mixer, lambda_init)
    hm = _oproj(hm[0], g, a_m, wo, layer, N_META)[None]
    q2, k, vt = _qkv_proj(hx, g, wqkv, *_rope_tables(N_META, seq), layer, tm)
    a = _attention_t(q2, k, vt, lam_p, subln_g, k_m, vt_m, mixer, tq, HEADS_PER_STEP, lambda_init)
    hx = _oproj(hx.reshape(bsz * seq, D_MODEL), g, a.reshape(bsz * seq, D_MODEL), wo, layer, tm)
    return hx.reshape(bsz, seq, D_MODEL), hm


ROW_TILE = 512
Q_TILE = 256
HEADS_PER_STEP = 8


def kernel(x, meta_tokens, ln_ffn1, ffn1_w_gu, ffn1_w_down, ln_mix, conv_w_in, conv_w, conv_w_out,
           attn_w_qkv, attn_lambda, attn_subln_g, attn_w_o, ln_ffn2, ffn2_w_gu, ffn2_w_down):
    bsz, seq, d = x.shape
    assert d == D_MODEL and seq % ROW_TILE == 0 and seq % Q_TILE == 0
    hx = x
    hm = meta_tokens.astype(x.dtype)[None]

    ffns = [(ln, wgu, wdn, i) for i in range(DEPTH)
            for ln, wgu, wdn in ((ln_ffn1, ffn1_w_gu, ffn1_w_down), (ln_ffn2, ffn2_w_gu, ffn2_w_down))]
    ffn_w16 = [ffns[0][1][0].astype(BF16), ffns[0][2][0].astype(BF16)]

    def run_ffn(n, hx, hm, mixer_w32=()):
        nonlocal ffn_w16
        ln, _, _, layer = ffns[n]
        casts = list(mixer_w32)
        if n + 1 < len(ffns):
            _, wgu32, wdn32, nxt = ffns[n + 1]
            casts += [(wgu32, nxt), (wdn32, nxt)]
        hx, hm, *w16 = _ffn(hx.reshape(bsz * seq, D_MODEL), hm[0], ln, *ffn_w16, layer, ROW_TILE, casts)
        ffn_w16 = w16[len(mixer_w32):]
        return hx.reshape(bsz, seq, D_MODEL), hm[None], w16[:len(mixer_w32)]

    no_history = jnp.zeros((CARRY_ROWS, D_MODEL), F32)
    for i in range(DEPTH):
        j = i // N_MIXERS
        if i % N_MIXERS == 0:
            hx, hm, (win, wout) = run_ffn(2 * i, hx, hm, [(conv_w_in, j), (conv_w_out, j)])
            hm, meta_tail = _conv_mixer(hm, ln_mix, no_history, win, conv_w, wout, i, j, N_META)
            hx, _ = _conv_mixer(hx, ln_mix, meta_tail[0], win, conv_w, wout, i, j, ROW_TILE)
        else:
            hx, hm, (wqkv, wo) = run_ffn(2 * i, hx, hm, [(attn_w_qkv, j), (attn_w_o, j)])
            lambda_init = 0.8 - 0.6 * math.exp(-0.3 * i)
            hx, hm = _attn_mixer(hx, hm, ln_mix, wqkv, attn_lambda, attn_subln_g, wo,
                                 i, j, ROW_TILE, Q_TILE, lambda_init)
        hx, hm, _ = run_ffn(2 * i + 1, hx, hm)
    return hx
```

```python
import functools
import math

import jax
import jax.numpy as jnp
from jax import lax
from jax.experimental import pallas as pl
from jax.experimental.pallas import tpu as pltpu

D_MODEL = 1024
DEPTH = 4
N_MIXERS = 2
N_META = 16
D_FF = 2816
CONV_WIDTH = 3
HEAD_DIM = 64
V_DIM = 2 * HEAD_DIM
N_HEADS = D_MODEL // V_DIM
ROT_DIM = HEAD_DIM // 4
ROPE_THETA = 500000.0
NORM_EPS = 1e-6
SUBLN_EPS = 1e-5
NEG_INF = -1e30

F32 = jnp.float32
BF16 = jnp.bfloat16

V7X_VMEM_LIMIT_BYTES = 60 * 1024 * 1024
SUBLANES = 8
LANES = 128
CARRY_ROWS = SUBLANES


def _rms(t, g, eps):
    return t * lax.rsqrt(jnp.mean(t * t, axis=-1, keepdims=True) + eps) * g


def _resident(shape):
    zeros = (0,) * len(shape)
    return pl.BlockSpec(shape, lambda *_: zeros, pipeline_mode=pl.Buffered(1))


def _layer_block(shape, layer):
    zeros = (0,) * len(shape)
    return pl.BlockSpec((None,) + tuple(shape), lambda *_: (layer,) + zeros, pipeline_mode=pl.Buffered(1))


def _params(n_grid_dims):
    return pltpu.CompilerParams(
        dimension_semantics=("arbitrary",) * n_grid_dims,
        vmem_limit_bytes=V7X_VMEM_LIMIT_BYTES)


FFN_SUB_TILES = 4
BF16_ROWS = 16


def _ffn_kernel(*refs, n_casts, mixer_out):
    hx_ref, hm_ref, g_ref, wgu_ref, wdn_ref = refs[:5]
    n_in = 5
    if mixer_out:
        ax_ref, am_ref, gmix_ref, wo_ref = refs[n_in:n_in + 4]
        n_in += 4
    w32_refs = refs[n_in:n_in + n_casts]
    ox_ref, om_ref = refs[n_in + n_casts:n_in + n_casts + 2]
    w16_refs = refs[n_in + n_casts + 2:]
    for w32_ref, w16_ref in zip(w32_refs, w16_refs):
        w16_ref[...] = w32_ref[...].astype(BF16)
    g = g_ref[...]
    sub = hx_ref.shape[0] // FFN_SUB_TILES

    def ffn(tiles, a_tiles):
        if mixer_out:
            ms = [jnp.dot(a, wo_ref[...], preferred_element_type=F32) for a in a_tiles]
            tiles = [h + _rms(m, gmix_ref[...][1:2], NORM_EPS) for h, m in zip(tiles, ms)]
        hns = [_rms(h, g[0:1], NORM_EPS).astype(BF16) for h in tiles]
        gus = [jnp.dot(hn, wgu_ref[...], preferred_element_type=F32) for hn in hns]
        acts = [(gu[:, :D_FF] * jax.nn.sigmoid(gu[:, :D_FF]) * gu[:, D_FF:]).astype(BF16) for gu in gus]
        fs = [jnp.dot(act, wdn_ref[...], preferred_element_type=F32) for act in acts]
        return [h + 0.5 * _rms(f, g[1:2], NORM_EPS) for h, f in zip(tiles, fs)]

    def real_tiles(x_ref):
        return [x_ref[s * sub:(s + 1) * sub] for s in range(FFN_SUB_TILES)]

    def store_real(outs):
        for s, o in enumerate(outs):
            ox_ref[s * sub:(s + 1) * sub] = o

    @pl.when(pl.program_id(0) == 0)
    def _():
        tiles = real_tiles(hx_ref)
        tiles[0] = jnp.concatenate([hm_ref[...], tiles[0]], axis=0)
        a_tiles = None
        if mixer_out:
            a_tiles = real_tiles(ax_ref)
            a_tiles[0] = jnp.concatenate([am_ref[...], a_tiles[0]], axis=0)
        outs = ffn(tiles, a_tiles)
        om_ref[...] = outs[0][:N_META]
        outs[0] = outs[0][N_META:]
        store_real(outs)

    @pl.when(pl.program_id(0) > 0)
    def _():
        store_real(ffn(real_tiles(hx_ref), real_tiles(ax_ref) if mixer_out else None))


def _slab_rows(n_rows, steps):
    for rows in range(BF16_ROWS, n_rows + 1, BF16_ROWS):
        if n_rows % rows == 0 and n_rows // rows <= steps:
            return rows
    raise ValueError((n_rows, steps))


def _ffn(hx, hm, g, wgu, wdn, layer, tm, casts=(), mixer_out=None):
    rows = hx.shape[0]
    steps = rows // tm
    tile = pl.BlockSpec((tm, D_MODEL), lambda i: (i, 0))
    in_specs = [tile, _resident((N_META, D_MODEL)), _layer_block((2, D_MODEL), layer),
                _resident((D_MODEL, 2 * D_FF)), _resident((D_FF, D_MODEL))]
    out_specs = [tile, pl.BlockSpec((N_META, D_MODEL), lambda i: (0, 0))]
    out_shape = [jax.ShapeDtypeStruct((rows, D_MODEL), F32), jax.ShapeDtypeStruct((N_META, D_MODEL), F32)]
    args = [hx, hm, g, wgu, wdn]
    if mixer_out is not None:
        in_specs += [tile, _resident((N_META, D_MODEL)), _layer_block((2, D_MODEL), layer),
                     _resident((D_MODEL, D_MODEL))]
        args += list(mixer_out)
    for w32, index in casts:
        _, n_rows, n_cols = w32.shape
        slab = _slab_rows(n_rows, steps)
        last = n_rows // slab - 1
        in_specs.append(pl.BlockSpec((None, slab, n_cols),
                                     lambda i, index=index, last=last: (index, jnp.minimum(i, last), 0)))
        out_specs.append(pl.BlockSpec((slab, n_cols), lambda i, last=last: (jnp.minimum(i, last), 0)))
        out_shape.append(jax.ShapeDtypeStruct((n_rows, n_cols), BF16))
    return pl.pallas_call(
        functools.partial(_ffn_kernel, n_casts=len(casts), mixer_out=mixer_out is not None),
        grid=(steps,),
        in_specs=in_specs,
        out_specs=out_specs,
        out_shape=out_shape,
        compiler_params=_params(1),
        name="ffn",
    )(*args, *[c[0] for c in casts])


MIXER_SUB_ROWS = 128


def _conv_kernel(h_ref, g_ref, tail_in_ref, win_ref, cw_ref, wout_ref,
                 o_ref, tail_out_ref, carry_ref, *, tm):
    @pl.when(pl.program_id(1) == 0)
    def _():
        carry_ref[...] = tail_in_ref[...]

    g = g_ref[...]
    cw = cw_ref[...]
    n_sub = max(1, tm // MIXER_SUB_ROWS)
    sub = tm // n_sub
    hs = [h_ref[0, s * sub:(s + 1) * sub] for s in range(n_sub)]
    hns = [_rms(h, g[0:1], NORM_EPS).astype(BF16) for h in hs]
    bcxs = [jnp.dot(hn, win_ref[...], preferred_element_type=F32) for hn in hns]
    row = lax.broadcasted_iota(jnp.int32, (sub, 1), 0)
    prev = carry_ref[...]
    ys = []
    for bcx in bcxs:
        u = bcx[:, D_MODEL:2 * D_MODEL] * bcx[:, 2 * D_MODEL:]
        last = prev[CARRY_ROWS - 1:CARRY_ROWS]
        u1 = jnp.where(row == 0, last, pltpu.roll(u, 1, 0))
        u2 = jnp.where(row == 0, prev[CARRY_ROWS - 2:CARRY_ROWS - 1],
                       jnp.where(row == 1, last, pltpu.roll(u, 2, 0)))
        conv = cw[0:1] * u2 + cw[1:2] * u1 + cw[2:3] * u
        ys.append((bcx[:, :D_MODEL] * conv).astype(BF16))
        prev = u[sub - CARRY_ROWS:]
    carry_ref[...] = prev
    tail_out_ref[0] = prev
    ms = [jnp.dot(y, wout_ref[...], preferred_element_type=F32) for y in ys]
    for s in range(n_sub):
        o_ref[0, s * sub:(s + 1) * sub] = hs[s] + _rms(ms[s], g[1:2], NORM_EPS)


def _conv_mixer(h, g, tail_in, win, cw, wout, layer, mixer, tm):
    bsz, seq, _ = h.shape
    return pl.pallas_call(
        functools.partial(_conv_kernel, tm=tm),
        grid=(bsz, seq // tm),
        in_specs=[
            pl.BlockSpec((1, tm, D_MODEL), lambda b, j: (b, j, 0)),
            _layer_block((2, D_MODEL), layer),
            _resident((CARRY_ROWS, D_MODEL)),
            _resident((D_MODEL, 3 * D_MODEL)),
            _layer_block((CONV_WIDTH, D_MODEL), mixer),
            _resident((D_MODEL, D_MODEL)),
        ],
        out_specs=[
            pl.BlockSpec((1, tm, D_MODEL), lambda b, j: (b, j, 0)),
            pl.BlockSpec((1, CARRY_ROWS, D_MODEL), lambda b, j: (b, 0, 0)),
        ],
        out_shape=[
            jax.ShapeDtypeStruct((bsz, seq, D_MODEL), F32),
            jax.ShapeDtypeStruct((bsz, CARRY_ROWS, D_MODEL), F32),
        ],
        scratch_shapes=[pltpu.VMEM((CARRY_ROWS, D_MODEL), F32)],
        compiler_params=_params(2),
        name="conv_mixer",
    )(h, g, tail_in, win, cw, wout)


NT_DIMS = (((1,), (1,)), ((), ()))

ONES_ROWS = 16
VT_ROWS = V_DIM + ONES_ROWS
Q_SCALE = HEAD_DIM ** -0.5 * math.log2(math.e)


def _qkv_kernel(h_ref, g_ref, w_ref, cos_ref, sin_lo_ref, sin_hi_ref, q2_ref, k_ref, vt_ref):
    g = g_ref[...]
    hn = _rms(h_ref[0], g[0:1], NORM_EPS).astype(BF16)
    qk = jnp.dot(hn, w_ref[...], preferred_element_type=F32)
    cos = cos_ref[...]
    sin_lo = sin_lo_ref[...]
    sin_hi = sin_hi_ref[...]
    half = ROT_DIM // 2
    first_component = lax.broadcasted_iota(jnp.int32, (1, LANES), 1) < HEAD_DIM
    for blk in range(2 * N_HEADS):
        x = qk[:, blk * LANES:(blk + 1) * LANES]
        r = x * cos + pltpu.roll(x, LANES - half, 1) * sin_lo + pltpu.roll(x, half, 1) * sin_hi
        if blk < N_HEADS:
            r = (r * Q_SCALE).astype(BF16)
            zero = jnp.zeros_like(r)
            q2_ref[0, 0, :, blk * LANES:(blk + 1) * LANES] = jnp.where(first_component, r, zero)
            q2_ref[0, 1, :, blk * LANES:(blk + 1) * LANES] = jnp.where(first_component, zero, r)
        else:
            k_ref[0, :, (blk - N_HEADS) * LANES:(blk - N_HEADS + 1) * LANES] = r.astype(BF16)
    ones = jnp.ones((ONES_ROWS, qk.shape[0]), BF16)
    for head in range(N_HEADS):
        v = qk[:, 2 * D_MODEL + head * V_DIM:2 * D_MODEL + (head + 1) * V_DIM]
        vt_ref[0, head, :V_DIM, :] = v.T.astype(BF16)
        vt_ref[0, head, V_DIM:, :] = ones


def _qkv_proj(h, g, w, cos, sin_lo, sin_hi, layer, tm):
    bsz, seq, _ = h.shape
    table = pl.BlockSpec((tm, LANES), lambda b, j: (j, 0))
    return pl.pallas_call(
        _qkv_kernel,
        grid=(bsz, seq // tm),
        in_specs=[
            pl.BlockSpec((1, tm, D_MODEL), lambda b, j: (b, j, 0)),
            _layer_block((2, D_MODEL), layer),
            _resident((D_MODEL, 3 * D_MODEL)),
            table, table, table,
        ],
        out_specs=[
            pl.BlockSpec((1, 2, tm, D_MODEL), lambda b, j: (b, 0, j, 0)),
            pl.BlockSpec((1, tm, D_MODEL), lambda b, j: (b, j, 0)),
            pl.BlockSpec((1, N_HEADS, VT_ROWS, tm), lambda b, j: (b, 0, 0, j)),
        ],
        out_shape=[
            jax.ShapeDtypeStruct((bsz, 2, seq, D_MODEL), BF16),
            jax.ShapeDtypeStruct((bsz, seq, D_MODEL), BF16),
            jax.ShapeDtypeStruct((bsz, N_HEADS, VT_ROWS, seq), BF16),
        ],
        compiler_params=_params(2),
        name="qkv_proj",
    )(h, g, w, cos, sin_lo, sin_hi)


SCORE_LOOKAHEAD = 2


def _attn_t_kernel(lam_ref, sg_ref, q2_ref, k_ref, vt_ref, kp_ref, vtp_ref, o_ref, acc_ref, st_ref,
                   *, tq, n_heads, lambda_init):
    seq = k_ref.shape[1]
    ahead = min(SCORE_LOOKAHEAD, n_heads)
    key_i = lax.broadcasted_iota(jnp.int32, (tq, 2 * tq), 0)
    qry_i = lax.broadcasted_iota(jnp.int32, (tq, 2 * tq), 1)
    causal = key_i <= jnp.where(qry_i >= tq, qry_i - tq, qry_i)
    lam_p = lam_ref[...]
    lam = (jnp.exp(jnp.sum(lam_p[0:1] * lam_p[1:2], axis=1, keepdims=True))
           - jnp.exp(jnp.sum(lam_p[2:3] * lam_p[3:4], axis=1, keepdims=True)) + lambda_init)
    sg = sg_ref[...]
    heads = [slice(g * LANES, (g + 1) * LANES) for g in range(n_heads)]

    def q_tile(qi, _):
        q_start = pl.multiple_of(qi * tq, tq)
        qqs = [jnp.concatenate([q2_ref[0, 0, pl.ds(q_start, tq), hs],
                                q2_ref[0, 1, pl.ds(q_start, tq), hs]], axis=0) for hs in heads]

        def scores(g, start):
            return lax.dot_general(k_ref[0, pl.ds(start, tq), heads[g]], qqs[g], NT_DIMS,
                                   preferred_element_type=F32)

        def first_scores(g):
            prefix = lax.dot_general(kp_ref[:, heads[g]], qqs[g], NT_DIMS, preferred_element_type=F32)
            return prefix, scores(g, q_start)

        def fold(start, next_start, maxes):
            first = maxes is None
            pending = [first_scores(g) for g in range(ahead)] if first else [None] * ahead
            new_maxes = []
            for g in range(n_heads):
                st = pending.pop(0)
                if st is None:
                    st = st_ref[g]
                upcoming = g + ahead
                if upcoming < n_heads:
                    pending.append(first_scores(upcoming) if first else scores(upcoming, start))
                else:
                    st_ref[upcoming - n_heads] = scores(upcoming - n_heads, next_start)
                vt = vt_ref[0, g, :, pl.ds(start, tq)]
                if first:
                    st_p, st = st
                    st = jnp.where(causal, st, NEG_INF)
                    m_new = jnp.maximum(jnp.max(st, axis=0, keepdims=True), jnp.max(st_p, axis=0, keepdims=True))
                    acc_ref[g] = (
                        jnp.dot(vtp_ref[g], jnp.exp2(st_p - m_new).astype(BF16), preferred_element_type=F32)
                        + jnp.dot(vt, jnp.exp2(st - m_new).astype(BF16), preferred_element_type=F32))
                else:
                    m_new = jnp.maximum(maxes[g], jnp.max(st, axis=0, keepdims=True))
                    alpha = jnp.exp2(maxes[g] - m_new)
                    pv = jnp.dot(vt, jnp.exp2(st - m_new).astype(BF16), preferred_element_type=F32)
                    acc_ref[g] = alpha * acc_ref[g] + pv
                new_maxes.append(m_new)
            return tuple(new_maxes)

        def block(kb, maxes):
            start = pl.multiple_of(kb * tq, tq)
            return fold(start, start + tq, maxes)

        lax.fori_loop(0, qi, block, fold(q_start, 0, None))

        for g, hs in enumerate(heads):
            acc = acc_ref[g]
            pv = acc[:V_DIM] * (1.0 / acc[V_DIM:V_DIM + 1])
            o = pv[:, :tq] - lam * pv[:, tq:]
            o = o * lax.rsqrt(jnp.mean(o * o, axis=0, keepdims=True) + SUBLN_EPS) * sg * (1.0 - lambda_init)
            o_ref[0, pl.ds(q_start, tq), hs] = o.T.astype(BF16)
        return 0

    lax.fori_loop(0, seq // tq, q_tile, 0)


def _attention_t(q2, k, vt, lam_p, subln_g, prefix_k, prefix_vt, mixer, tq, n_heads, lambda_init):
    bsz, seq, _ = k.shape
    n_prefix = prefix_k.shape[0]
    groups = N_HEADS // n_heads
    width = n_heads * LANES
    return pl.pallas_call(
        functools.partial(_attn_t_kernel, tq=tq, n_heads=n_heads, lambda_init=lambda_init),
        grid=(bsz, groups),
        in_specs=[
            _layer_block((4, HEAD_DIM), mixer),
            _layer_block((V_DIM, 1), mixer),
            pl.BlockSpec((1, 2, seq, width), lambda b, h: (b, 0, 0, h)),
            pl.BlockSpec((1, seq, width), lambda b, h: (b, 0, h)),
            pl.BlockSpec((1, n_heads, VT_ROWS, seq), lambda b, h: (b, h, 0, 0)),
            pl.BlockSpec((n_prefix, width), lambda b, h: (0, h)),
            pl.BlockSpec((n_heads, VT_ROWS, n_prefix), lambda b, h: (h, 0, 0)),
        ],
        out_specs=pl.BlockSpec((1, seq, width), lambda b, h: (b, 0, h)),
        out_shape=jax.ShapeDtypeStruct((bsz, seq, D_MODEL), BF16),
        scratch_shapes=[pltpu.VMEM((n_heads, VT_ROWS, 2 * tq), F32),
                        pltpu.VMEM((min(SCORE_LOOKAHEAD, n_heads), tq, 2 * tq), F32)],
        compiler_params=_params(2),
        name="diff_attention",
    )(lam_p, subln_g.reshape(-1, V_DIM, 1), q2, k, vt, prefix_k, prefix_vt)


def _attn_small_kernel(lam_ref, sg_ref, q2_ref, k_ref, v_ref, o_ref, *, lambda_init):
    tq = k_ref.shape[0]
    qq = jnp.concatenate([q2_ref[0], q2_ref[1]], axis=0)
    s = lax.dot_general(qq, k_ref[...], NT_DIMS, preferred_element_type=F32)
    r = lax.broadcasted_iota(jnp.int32, (2 * tq, tq), 0)
    c = lax.broadcasted_iota(jnp.int32, (2 * tq, tq), 1)
    s = jnp.where(c <= jnp.where(r >= tq, r - tq, r), s, NEG_INF)
    p = jnp.exp2(s - jnp.max(s, axis=1, keepdims=True))
    l = jnp.sum(p, axis=1, keepdims=True)
    pv = jnp.dot(p.astype(BF16), v_ref[...], preferred_element_type=F32) / l
    lam_p = lam_ref[...]
    lam = (jnp.exp(jnp.sum(lam_p[0:1] * lam_p[1:2], axis=1, keepdims=True))
           - jnp.exp(jnp.sum(lam_p[2:3] * lam_p[3:4], axis=1, keepdims=True)) + lambda_init)
    o = pv[:tq] - lam * pv[tq:]
    o_ref[...] = (_rms(o, sg_ref[...], SUBLN_EPS) * (1.0 - lambda_init)).astype(BF16)


def _attention_small(q2, k, v, lam_p, subln_g, mixer, lambda_init):
    seq = k.shape[0]
    return pl.pallas_call(
        functools.partial(_attn_small_kernel, lambda_init=lambda_init),
        grid=(N_HEADS,),
        in_specs=[
            _layer_block((4, HEAD_DIM), mixer),
            _layer_block((1, V_DIM), mixer),
            pl.BlockSpec((2, seq, LANES), lambda h: (0, 0, h)),
            pl.BlockSpec((seq, LANES), lambda h: (0, h)),
            pl.BlockSpec((seq, LANES), lambda h: (0, h)),
        ],
        out_specs=pl.BlockSpec((seq, LANES), lambda h: (0, h)),
        out_shape=jax.ShapeDtypeStruct((seq, D_MODEL), BF16),
        compiler_params=_params(1),
        name="diff_attention_meta",
    )(lam_p, subln_g.reshape(-1, 1, V_DIM), q2, k, v)


def _rope_tables(first_pos, n_pos):
    half = ROT_DIM // 2
    pos = jnp.arange(first_pos, first_pos + n_pos, dtype=F32)
    inv_freq = ROPE_THETA ** (-jnp.arange(0, ROT_DIM, 2, dtype=F32) / ROT_DIM)
    ang = pos[:, None] * inv_freq[None, :]
    cos, sin = jnp.cos(ang), jnp.sin(ang)
    ones = jnp.ones((n_pos, HEAD_DIM - ROT_DIM), F32)
    zeros = jnp.zeros((n_pos, HEAD_DIM - half), F32)
    cos64 = jnp.concatenate([cos, cos, ones], axis=1)
    lo64 = jnp.concatenate([-sin, zeros], axis=1)
    hi64 = jnp.concatenate([jnp.zeros((n_pos, half), F32), sin, ones * 0.0], axis=1)
    return tuple(jnp.concatenate([t, t], axis=1) for t in (cos64, lo64, hi64))


def _attn_mixer(hx, hm, g, wqkv, lam_p, subln_g, layer, mixer, tm, tq, lambda_init):
    bsz, seq, _ = hx.shape
    q2_m, k_m, vt_m = (t[0] for t in _qkv_proj(hm, g, wqkv, *_rope_tables(0, N_META), layer, N_META))
    v_m = jnp.transpose(vt_m[:, :V_DIM, :], (2, 0, 1)).reshape(N_META, D_MODEL)
    a_m = _attention_small(q2_m, k_m, v_m, lam_p, subln_g, mixer, lambda_init)
    q2, k, vt = _qkv_proj(hx, g, wqkv, *_rope_tables(N_META, seq), layer, tm)
    a = _attention_t(q2, k, vt, lam_p, subln_g, k_m, vt_m, mixer, tq, HEADS_PER_STEP, lambda_init)
    return a, a_m


ROW_TILE = 512
FFN_ROW_TILE = 512
Q_TILE = 256
HEADS_PER_STEP = 8


def kernel(x, meta_tokens, ln_ffn1, ffn1_w_gu, ffn1_w_down, ln_mix, conv_w_in, conv_w, conv_w_out,
           attn_w_qkv, attn_lambda, attn_subln_g, attn_w_o, ln_ffn2, ffn2_w_gu, ffn2_w_down):
    bsz, seq, d = x.shape
    assert d == D_MODEL and seq % ROW_TILE == 0 and seq % Q_TILE == 0 and (bsz * seq) % FFN_ROW_TILE == 0
    hx = x
    hm = meta_tokens.astype(x.dtype)[None]

    ffns = [(ln, wgu, wdn, i) for i in range(DEPTH)
            for ln, wgu, wdn in ((ln_ffn1, ffn1_w_gu, ffn1_w_down), (ln_ffn2, ffn2_w_gu, ffn2_w_down))]
    ffn_w16 = [ffns[0][1][0].astype(BF16), ffns[0][2][0].astype(BF16)]

    def run_ffn(n, hx, hm, mixer_w32=(), mixer_out=None):
        nonlocal ffn_w16
        ln, _, _, layer = ffns[n]
        casts = list(mixer_w32)
        if n + 1 < len(ffns):
            _, wgu32, wdn32, nxt = ffns[n + 1]
            casts += [(wgu32, nxt), (wdn32, nxt)]
        hx, hm, *w16 = _ffn(hx.reshape(bsz * seq, D_MODEL), hm[0], ln, *ffn_w16, layer, FFN_ROW_TILE, casts,
                            mixer_out)
        ffn_w16 = w16[len(mixer_w32):]
        return hx.reshape(bsz, seq, D_MODEL), hm[None], w16[:len(mixer_w32)]

    no_history = jnp.zeros((CARRY_ROWS, D_MODEL), F32)
    for i in range(DEPTH):
        j = i // N_MIXERS
        if i % N_MIXERS == 0:
            hx, hm, (win, wout) = run_ffn(2 * i, hx, hm, [(conv_w_in, j), (conv_w_out, j)])
            hm, meta_tail = _conv_mixer(hm, ln_mix, no_history, win, conv_w, wout, i, j, N_META)
            hx, _ = _conv_mixer(hx, ln_mix, meta_tail[0], win, conv_w, wout, i, j, ROW_TILE)
            mixer_out = None
        else:
            hx, hm, (wqkv, wo) = run_ffn(2 * i, hx, hm, [(attn_w_qkv, j), (attn_w_o, j)])
            lambda_init = 0.8 - 0.6 * math.exp(-0.3 * i)
            a, a_m = _attn_mixer(hx, hm, ln_mix, wqkv, attn_lambda, attn_subln_g,
                                 i, j, ROW_TILE, Q_TILE, lambda_init)
            mixer_out = (a.reshape(bsz * seq, D_MODEL), a_m, ln_mix, wo)
        hx, hm, _ = run_ffn(2 * i + 1, hx, hm, mixer_out=mixer_out)
    return hx
```

```python
import functools
import math

import jax
import jax.numpy as jnp
from jax import lax
from jax.experimental import pallas as pl
from jax.experimental.pallas import tpu as pltpu

D_MODEL = 1024
DEPTH = 4
N_MIXERS = 2
N_META = 16
D_FF = 2816
CONV_WIDTH = 3
HEAD_DIM = 64
V_DIM = 2 * HEAD_DIM
N_HEADS = D_MODEL // V_DIM
ROT_DIM = HEAD_DIM // 4
ROPE_THETA = 500000.0
NORM_EPS = 1e-6
SUBLN_EPS = 1e-5
NEG_INF = -1e30

F32 = jnp.float32
BF16 = jnp.bfloat16

V7X_VMEM_LIMIT_BYTES = 60 * 1024 * 1024
SUBLANES = 8
LANES = 128
CARRY_ROWS = SUBLANES


def _rms(t, g, eps):
    return t * lax.rsqrt(jnp.mean(t * t, axis=-1, keepdims=True) + eps) * g


def _resident(shape):
    zeros = (0,) * len(shape)
    return pl.BlockSpec(shape, lambda *_: zeros, pipeline_mode=pl.Buffered(1))


def _layer_block(shape, layer):
    zeros = (0,) * len(shape)
    return pl.BlockSpec((None,) + tuple(shape), lambda *_: (layer,) + zeros, pipeline_mode=pl.Buffered(1))


def _params(n_grid_dims):
    return pltpu.CompilerParams(
        dimension_semantics=("arbitrary",) * n_grid_dims,
        vmem_limit_bytes=V7X_VMEM_LIMIT_BYTES)


FFN_SUB_ROWS = 128
BF16_ROWS = 16


def _ffn_kernel(*refs, n_casts, mixer_out):
    hx_ref, hm_ref, g_ref, wgu_ref, wdn_ref = refs[:5]
    n_in = 5
    if mixer_out:
        ax_ref, am_ref, gmix_ref, wo_ref = refs[n_in:n_in + 4]
        n_in += 4
    w32_refs = refs[n_in:n_in + n_casts]
    ox_ref, om_ref = refs[n_in + n_casts:n_in + n_casts + 2]
    w16_refs = refs[n_in + n_casts + 2:]
    for w32_ref, w16_ref in zip(w32_refs, w16_refs):
        w16_ref[...] = w32_ref[...].astype(BF16)
    g = g_ref[...]
    sub = FFN_SUB_ROWS
    n_sub = hx_ref.shape[0] // sub

    def ffn(tiles, a_tiles):
        if mixer_out:
            ms = [jnp.dot(a, wo_ref[...], preferred_element_type=F32) for a in a_tiles]
            tiles = [h + _rms(m, gmix_ref[...][1:2], NORM_EPS) for h, m in zip(tiles, ms)]
        hns = [_rms(h, g[0:1], NORM_EPS).astype(BF16) for h in tiles]
        gus = [jnp.dot(hn, wgu_ref[...], preferred_element_type=F32) for hn in hns]
        acts = [(gu[:, :D_FF] * jax.nn.sigmoid(gu[:, :D_FF]) * gu[:, D_FF:]).astype(BF16) for gu in gus]
        fs = [jnp.dot(act, wdn_ref[...], preferred_element_type=F32) for act in acts]
        return [h + 0.5 * _rms(f, g[1:2], NORM_EPS) for h, f in zip(tiles, fs)]

    def real_tiles(x_ref):
        return [x_ref[s * sub:(s + 1) * sub] for s in range(n_sub)]

    def store_real(outs):
        for s, o in enumerate(outs):
            ox_ref[s * sub:(s + 1) * sub] = o

    @pl.when(pl.program_id(0) == 0)
    def _():
        tiles = real_tiles(hx_ref)
        tiles[0] = jnp.concatenate([hm_ref[...], tiles[0]], axis=0)
        a_tiles = None
        if mixer_out:
            a_tiles = real_tiles(ax_ref)
            a_tiles[0] = jnp.concatenate([am_ref[...], a_tiles[0]], axis=0)
        outs = ffn(tiles, a_tiles)
        om_ref[...] = outs[0][:N_META]
        outs[0] = outs[0][N_META:]
        store_real(outs)

    @pl.when(pl.program_id(0) > 0)
    def _():
        store_real(ffn(real_tiles(hx_ref), real_tiles(ax_ref) if mixer_out else None))


def _slab_rows(n_rows, steps):
    for rows in range(BF16_ROWS, n_rows + 1, BF16_ROWS):
        if n_rows % rows == 0 and n_rows // rows <= steps:
            return rows
    raise ValueError((n_rows, steps))


def _ffn(hx, hm, g, wgu, wdn, layer, tm, casts=(), mixer_out=None):
    rows = hx.shape[0]
    steps = rows // tm
    tile = pl.BlockSpec((tm, D_MODEL), lambda i: (i, 0))
    in_specs = [tile, _resident((N_META, D_MODEL)), _layer_block((2, D_MODEL), layer),
                _resident((D_MODEL, 2 * D_FF)), _resident((D_FF, D_MODEL))]
    out_specs = [tile, pl.BlockSpec((N_META, D_MODEL), lambda i: (0, 0))]
    out_shape = [jax.ShapeDtypeStruct((rows, D_MODEL), F32), jax.ShapeDtypeStruct((N_META, D_MODEL), F32)]
    args = [hx, hm, g, wgu, wdn]
    if mixer_out is not None:
        in_specs += [tile, _resident((N_META, D_MODEL)), _layer_block((2, D_MODEL), layer),
                     _resident((D_MODEL, D_MODEL))]
        args += list(mixer_out)
    for w32, index in casts:
        _, n_rows, n_cols = w32.shape
        slab = _slab_rows(n_rows, steps)
        last = n_rows // slab - 1
        in_specs.append(pl.BlockSpec((None, slab, n_cols),
                                     lambda i, index=index, last=last: (index, jnp.minimum(i, last), 0)))
        out_specs.append(pl.BlockSpec((slab, n_cols), lambda i, last=last: (jnp.minimum(i, last), 0)))
        out_shape.append(jax.ShapeDtypeStruct((n_rows, n_cols), BF16))
    return pl.pallas_call(
        functools.partial(_ffn_kernel, n_casts=len(casts), mixer_out=mixer_out is not None),
        grid=(steps,),
        in_specs=in_specs,
        out_specs=out_specs,
        out_shape=out_shape,
        compiler_params=_params(1),
        name="ffn",
    )(*args, *[c[0] for c in casts])


MIXER_SUB_ROWS = 128


def _conv_kernel(h_ref, g_ref, tail_in_ref, win_ref, cw_ref, wout_ref,
                 o_ref, tail_out_ref, carry_ref, *, tm):
    @pl.when(pl.program_id(1) == 0)
    def _():
        carry_ref[...] = tail_in_ref[...]

    g = g_ref[...]
    cw = cw_ref[...]
    n_sub = max(1, tm // MIXER_SUB_ROWS)
    sub = tm // n_sub
    hs = [h_ref[0, s * sub:(s + 1) * sub] for s in range(n_sub)]
    hns = [_rms(h, g[0:1], NORM_EPS).astype(BF16) for h in hs]
    bcxs = [jnp.dot(hn, win_ref[...], preferred_element_type=F32) for hn in hns]
    row = lax.broadcasted_iota(jnp.int32, (sub, 1), 0)
    prev = carry_ref[...]
    ys = []
    for bcx in bcxs:
        u = bcx[:, D_MODEL:2 * D_MODEL] * bcx[:, 2 * D_MODEL:]
        last = prev[CARRY_ROWS - 1:CARRY_ROWS]
        u1 = jnp.where(row == 0, last, pltpu.roll(u, 1, 0))
        u2 = jnp.where(row == 0, prev[CARRY_ROWS - 2:CARRY_ROWS - 1],
                       jnp.where(row == 1, last, pltpu.roll(u, 2, 0)))
        conv = cw[0:1] * u2 + cw[1:2] * u1 + cw[2:3] * u
        ys.append((bcx[:, :D_MODEL] * conv).astype(BF16))
        prev = u[sub - CARRY_ROWS:]
    carry_ref[...] = prev
    tail_out_ref[0] = prev
    ms = [jnp.dot(y, wout_ref[...], preferred_element_type=F32) for y in ys]
    for s in range(n_sub):
        o_ref[0, s * sub:(s + 1) * sub] = hs[s] + _rms(ms[s], g[1:2], NORM_EPS)


def _conv_mixer(h, g, tail_in, win, cw, wout, layer, mixer, tm):
    bsz, seq, _ = h.shape
    return pl.pallas_call(
        functools.partial(_conv_kernel, tm=tm),
        grid=(bsz, seq // tm),
        in_specs=[
            pl.BlockSpec((1, tm, D_MODEL), lambda b, j: (b, j, 0)),
            _layer_block((2, D_MODEL), layer),
            _resident((CARRY_ROWS, D_MODEL)),
            _resident((D_MODEL, 3 * D_MODEL)),
            _layer_block((CONV_WIDTH, D_MODEL), mixer),
            _resident((D_MODEL, D_MODEL)),
        ],
        out_specs=[
            pl.BlockSpec((1, tm, D_MODEL), lambda b, j: (b, j, 0)),
            pl.BlockSpec((1, CARRY_ROWS, D_MODEL), lambda b, j: (b, 0, 0)),
        ],
        out_shape=[
            jax.ShapeDtypeStruct((bsz, seq, D_MODEL), F32),
            jax.ShapeDtypeStruct((bsz, CARRY_ROWS, D_MODEL), F32),
        ],
        scratch_shapes=[pltpu.VMEM((CARRY_ROWS, D_MODEL), F32)],
        compiler_params=_params(2),
        name="conv_mixer",
    )(h, g, tail_in, win, cw, wout)


NT_DIMS = (((1,), (1,)), ((), ()))

ONES_ROWS = 16
VT_ROWS = V_DIM + ONES_ROWS
Q_SCALE = HEAD_DIM ** -0.5 * math.log2(math.e)


def _qkv_kernel(h_ref, g_ref, w_ref, cos_ref, sin_lo_ref, sin_hi_ref, q2_ref, k_ref, vt_ref):
    g = g_ref[...]
    hn = _rms(h_ref[0], g[0:1], NORM_EPS).astype(BF16)
    qk = jnp.dot(hn, w_ref[...], preferred_element_type=F32)
    cos = cos_ref[...]
    sin_lo = sin_lo_ref[...]
    sin_hi = sin_hi_ref[...]
    half = ROT_DIM // 2
    first_component = lax.broadcasted_iota(jnp.int32, (1, LANES), 1) < HEAD_DIM
    for blk in range(2 * N_HEADS):
        x = qk[:, blk * LANES:(blk + 1) * LANES]
        r = x * cos + pltpu.roll(x, LANES - half, 1) * sin_lo + pltpu.roll(x, half, 1) * sin_hi
        if blk < N_HEADS:
            r = (r * Q_SCALE).astype(BF16)
            zero = jnp.zeros_like(r)
            q2_ref[0, 0, :, blk * LANES:(blk + 1) * LANES] = jnp.where(first_component, r, zero)
            q2_ref[0, 1, :, blk * LANES:(blk + 1) * LANES] = jnp.where(first_component, zero, r)
        else:
            k_ref[0, :, (blk - N_HEADS) * LANES:(blk - N_HEADS + 1) * LANES] = r.astype(BF16)
    ones = jnp.ones((ONES_ROWS, qk.shape[0]), BF16)
    for head in range(N_HEADS):
        v = qk[:, 2 * D_MODEL + head * V_DIM:2 * D_MODEL + (head + 1) * V_DIM]
        vt_ref[0, head, :V_DIM, :] = v.T.astype(BF16)
        vt_ref[0, head, V_DIM:, :] = ones


def _qkv_proj(h, g, w, cos, sin_lo, sin_hi, layer, tm):
    bsz, seq, _ = h.shape
    table = pl.BlockSpec((tm, LANES), lambda b, j: (j, 0))
    return pl.pallas_call(
        _qkv_kernel,
        grid=(bsz, seq // tm),
        in_specs=[
            pl.BlockSpec((1, tm, D_MODEL), lambda b, j: (b, j, 0)),
            _layer_block((2, D_MODEL), layer),
            _resident((D_MODEL, 3 * D_MODEL)),
            table, table, table,
        ],
        out_specs=[
            pl.BlockSpec((1, 2, tm, D_MODEL), lambda b, j: (b, 0, j, 0)),
            pl.BlockSpec((1, tm, D_MODEL), lambda b, j: (b, j, 0)),
            pl.BlockSpec((1, N_HEADS, VT_ROWS, tm), lambda b, j: (b, 0, 0, j)),
        ],
        out_shape=[
            jax.ShapeDtypeStruct((bsz, 2, seq, D_MODEL), BF16),
            jax.ShapeDtypeStruct((bsz, seq, D_MODEL), BF16),
            jax.ShapeDtypeStruct((bsz, N_HEADS, VT_ROWS, seq), BF16),
        ],
        compiler_params=_params(2),
        name="qkv_proj",
    )(h, g, w, cos, sin_lo, sin_hi)


SCORE_LOOKAHEAD = 2


def _attn_t_kernel(lam_ref, sg_ref, q2_ref, k_ref, vt_ref, kp_ref, vtp_ref, o_ref, acc_ref, st_ref,
                   *, tq, n_heads, lambda_init):
    seq = k_ref.shape[1]
    ahead = min(SCORE_LOOKAHEAD, n_heads)
    key_i = lax.broadcasted_iota(jnp.int32, (tq, 2 * tq), 0)
    qry_i = lax.broadcasted_iota(jnp.int32, (tq, 2 * tq), 1)
    causal = key_i <= jnp.where(qry_i >= tq, qry_i - tq, qry_i)
    lam_p = lam_ref[...]
    lam = (jnp.exp(jnp.sum(lam_p[0:1] * lam_p[1:2], axis=1, keepdims=True))
           - jnp.exp(jnp.sum(lam_p[2:3] * lam_p[3:4], axis=1, keepdims=True)) + lambda_init)
    sg = sg_ref[...]
    heads = [slice(g * LANES, (g + 1) * LANES) for g in range(n_heads)]

    def q_tile(qi, _):
        q_start = pl.multiple_of(qi * tq, tq)
        qqs = [jnp.concatenate([q2_ref[0, 0, pl.ds(q_start, tq), hs],
                                q2_ref[0, 1, pl.ds(q_start, tq), hs]], axis=0) for hs in heads]

        def scores(g, start):
            return lax.dot_general(k_ref[0, pl.ds(start, tq), heads[g]], qqs[g], NT_DIMS,
                                   preferred_element_type=F32)

        def first_scores(g):
            prefix = lax.dot_general(kp_ref[:, heads[g]], qqs[g], NT_DIMS, preferred_element_type=F32)
            return prefix, scores(g, q_start)

        def fold(start, next_start, maxes):
            first = maxes is None
            pending = [first_scores(g) for g in range(ahead)] if first else [None] * ahead
            new_maxes = []
            for g in range(n_heads):
                st = pending.pop(0)
                if st is None:
                    st = st_ref[g]
                upcoming = g + ahead
                if upcoming < n_heads:
                    pending.append(first_scores(upcoming) if first else scores(upcoming, start))
                else:
                    st_ref[upcoming - n_heads] = scores(upcoming - n_heads, next_start)
                vt = vt_ref[0, g, :, pl.ds(start, tq)]
                if first:
                    st_p, st = st
                    st = jnp.where(causal, st, NEG_INF)
                    m_new = jnp.maximum(jnp.max(st, axis=0, keepdims=True), jnp.max(st_p, axis=0, keepdims=True))
                    acc_ref[g] = (
                        jnp.dot(vtp_ref[g], jnp.exp2(st_p - m_new).astype(BF16), preferred_element_type=F32)
                        + jnp.dot(vt, jnp.exp2(st - m_new).astype(BF16), preferred_element_type=F32))
                else:
                    m_new = jnp.maximum(maxes[g], jnp.max(st, axis=0, keepdims=True))
                    alpha = jnp.exp2(maxes[g] - m_new)
                    pv = jnp.dot(vt, jnp.exp2(st - m_new).astype(BF16), preferred_element_type=F32)
                    acc_ref[g] = alpha * acc_ref[g] + pv
                new_maxes.append(m_new)
            return tuple(new_maxes)

        def block(kb, maxes):
            start = pl.multiple_of(kb * tq, tq)
            return fold(start, start + tq, maxes)

        lax.fori_loop(0, qi, block, fold(q_start, 0, None))

        for g, hs in enumerate(heads):
            acc = acc_ref[g]
            pv = acc[:V_DIM] * (1.0 / acc[V_DIM:V_DIM + 1])
            o = pv[:, :tq] - lam * pv[:, tq:]
            o = o * lax.rsqrt(jnp.mean(o * o, axis=0, keepdims=True) + SUBLN_EPS) * sg * (1.0 - lambda_init)
            o_ref[0, pl.ds(q_start, tq), hs] = o.T.astype(BF16)
        return 0

    lax.fori_loop(0, seq // tq, q_tile, 0)


def _attention_t(q2, k, vt, lam_p, subln_g, prefix_k, prefix_vt, mixer, tq, n_heads, lambda_init):
    bsz, seq, _ = k.shape
    n_prefix = prefix_k.shape[0]
    groups = N_HEADS // n_heads
    width = n_heads * LANES
    return pl.pallas_call(
        functools.partial(_attn_t_kernel, tq=tq, n_heads=n_heads, lambda_init=lambda_init),
        grid=(bsz, groups),
        in_specs=[
            _layer_block((4, HEAD_DIM), mixer),
            _layer_block((V_DIM, 1), mixer),
            pl.BlockSpec((1, 2, seq, width), lambda b, h: (b, 0, 0, h)),
            pl.BlockSpec((1, seq, width), lambda b, h: (b, 0, h)),
            pl.BlockSpec((1, n_heads, VT_ROWS, seq), lambda b, h: (b, h, 0, 0)),
            pl.BlockSpec((n_prefix, width), lambda b, h: (0, h)),
            pl.BlockSpec((n_heads, VT_ROWS, n_prefix), lambda b, h: (h, 0, 0)),
        ],
        out_specs=pl.BlockSpec((1, seq, width), lambda b, h: (b, 0, h)),
        out_shape=jax.ShapeDtypeStruct((bsz, seq, D_MODEL), BF16),
        scratch_shapes=[pltpu.VMEM((n_heads, VT_ROWS, 2 * tq), F32),
                        pltpu.VMEM((min(SCORE_LOOKAHEAD, n_heads), tq, 2 * tq), F32)],
        compiler_params=_params(2),
        name="diff_attention",
    )(lam_p, subln_g.reshape(-1, V_DIM, 1), q2, k, vt, prefix_k, prefix_vt)


def _attn_small_kernel(lam_ref, sg_ref, q2_ref, k_ref, v_ref, o_ref, *, lambda_init):
    tq = k_ref.shape[0]
    qq = jnp.concatenate([q2_ref[0], q2_ref[1]], axis=0)
    s = lax.dot_general(qq, k_ref[...], NT_DIMS, preferred_element_type=F32)
    r = lax.broadcasted_iota(jnp.int32, (2 * tq, tq), 0)
    c = lax.broadcasted_iota(jnp.int32, (2 * tq, tq), 1)
    s = jnp.where(c <= jnp.where(r >= tq, r - tq, r), s, NEG_INF)
    p = jnp.exp2(s - jnp.max(s, axis=1, keepdims=True))
    l = jnp.sum(p, axis=1, keepdims=True)
    pv = jnp.dot(p.astype(BF16), v_ref[...], preferred_element_type=F32) / l
    lam_p = lam_ref[...]
    lam = (jnp.exp(jnp.sum(lam_p[0:1] * lam_p[1:2], axis=1, keepdims=True))
           - jnp.exp(jnp.sum(lam_p[2:3] * lam_p[3:4], axis=1, keepdims=True)) + lambda_init)
    o = pv[:tq] - lam * pv[tq:]
    o_ref[...] = (_rms(o, sg_ref[...], SUBLN_EPS) * (1.0 - lambda_init)).astype(BF16)


def _attention_small(q2, k, v, lam_p, subln_g, mixer, lambda_init):
    seq = k.shape[0]
    return pl.pallas_call(
        functools.partial(_attn_small_kernel, lambda_init=lambda_init),
        grid=(N_HEADS,),
        in_specs=[
            _layer_block((4, HEAD_DIM), mixer),
            _layer_block((1, V_DIM), mixer),
            pl.BlockSpec((2, seq, LANES), lambda h: (0, 0, h)),
            pl.BlockSpec((seq, LANES), lambda h: (0, h)),
            pl.BlockSpec((seq, LANES), lambda h: (0, h)),
        ],
        out_specs=pl.BlockSpec((seq, LANES), lambda h: (0, h)),
        out_shape=jax.ShapeDtypeStruct((seq, D_MODEL), BF16),
        compiler_params=_params(1),
        name="diff_attention_meta",
    )(lam_p, subln_g.reshape(-1, 1, V_DIM), q2, k, v)


def _rope_tables(first_pos, n_pos):
    half = ROT_DIM // 2
    pos = jnp.arange(first_pos, first_pos + n_pos, dtype=F32)
    inv_freq = ROPE_THETA ** (-jnp.arange(0, ROT_DIM, 2, dtype=F32) / ROT_DIM)
    ang = pos[:, None] * inv_freq[None, :]
    cos, sin = jnp.cos(ang), jnp.sin(ang)
    ones = jnp.ones((n_pos, HEAD_DIM - ROT_DIM), F32)
    zeros = jnp.zeros((n_pos, HEAD_DIM - half), F32)
    cos64 = jnp.concatenate([cos, cos, ones], axis=1)
    lo64 = jnp.concatenate([-sin, zeros], axis=1)
    hi64 = jnp.concatenate([jnp.zeros((n_pos, half), F32), sin, ones * 0.0], axis=1)
    return tuple(jnp.concatenate([t, t], axis=1) for t in (cos64, lo64, hi64))


def _attn_mixer(hx, hm, g, wqkv, lam_p, subln_g, layer, mixer, tm, tq, lambda_init):
    bsz, seq, _ = hx.shape
    q2_m, k_m, vt_m = (t[0] for t in _qkv_proj(hm, g, wqkv, *_rope_tables(0, N_META), layer, N_META))
    v_m = jnp.transpose(vt_m[:, :V_DIM, :], (2, 0, 1)).reshape(N_META, D_MODEL)
    a_m = _attention_small(q2_m, k_m, v_m, lam_p, subln_g, mixer, lambda_init)
    q2, k, vt = _qkv_proj(hx, g, wqkv, *_rope_tables(N_META, seq), layer, tm)
    a = _attention_t(q2, k, vt, lam_p, subln_g, k_m, vt_m, mixer, tq, HEADS_PER_STEP, lambda_init)
    return a, a_m


ROW_TILE = 512
FFN_ROW_TILE = 1024
FFN_FUSED_ROW_TILE = 512
Q_TILE = 256
HEADS_PER_STEP = 8


def kernel(x, meta_tokens, ln_ffn1, ffn1_w_gu, ffn1_w_down, ln_mix, conv_w_in, conv_w, conv_w_out,
           attn_w_qkv, attn_lambda, attn_subln_g, attn_w_o, ln_ffn2, ffn2_w_gu, ffn2_w_down):
    bsz, seq, d = x.shape
    assert d == D_MODEL and seq % ROW_TILE == 0 and seq % Q_TILE == 0 and (bsz * seq) % FFN_ROW_TILE == 0
    hx = x
    hm = meta_tokens.astype(x.dtype)[None]

    ffns = [(ln, wgu, wdn, i) for i in range(DEPTH)
            for ln, wgu, wdn in ((ln_ffn1, ffn1_w_gu, ffn1_w_down), (ln_ffn2, ffn2_w_gu, ffn2_w_down))]
    ffn_w16 = [ffns[0][1][0].astype(BF16), ffns[0][2][0].astype(BF16)]
    mixer_w32 = [[(conv_w_in, i // N_MIXERS), (conv_w_out, i // N_MIXERS)] if i % N_MIXERS == 0 else
                 [(attn_w_qkv, i // N_MIXERS), (attn_w_o, i // N_MIXERS)] for i in range(DEPTH)]
    mixer_w16 = {}

    def run_ffn(n, hx, hm, mixer_out=None):
        nonlocal ffn_w16
        ln, _, _, layer = ffns[n]
        mixer_layer = 0 if n == 0 else (n + 1) // 2 if n % 2 == 1 and (n + 1) // 2 < DEPTH else None
        casts = list(mixer_w32[mixer_layer]) if mixer_layer is not None else []
        n_mixer = len(casts)
        if n + 1 < len(ffns):
            _, wgu32, wdn32, nxt = ffns[n + 1]
            casts += [(wgu32, nxt), (wdn32, nxt)]
        hx, hm, *w16 = _ffn(hx.reshape(bsz * seq, D_MODEL), hm[0], ln, *ffn_w16, layer,
                            FFN_ROW_TILE if mixer_out is None else FFN_FUSED_ROW_TILE, casts, mixer_out)
        if mixer_layer is not None:
            mixer_w16[mixer_layer] = w16[:n_mixer]
        ffn_w16 = w16[n_mixer:]
        return hx.reshape(bsz, seq, D_MODEL), hm[None]

    no_history = jnp.zeros((CARRY_ROWS, D_MODEL), F32)
    for i in range(DEPTH):
        j = i // N_MIXERS
        hx, hm = run_ffn(2 * i, hx, hm)
        if i % N_MIXERS == 0:
            win, wout = mixer_w16[i]
            hm, meta_tail = _conv_mixer(hm, ln_mix, no_history, win, conv_w, wout, i, j, N_META)
            hx, _ = _conv_mixer(hx, ln_mix, meta_tail[0], win, conv_w, wout, i, j, ROW_TILE)
            mixer_out = None
        else:
            wqkv, wo = mixer_w16[i]
            lambda_init = 0.8 - 0.6 * math.exp(-0.3 * i)
            a, a_m = _attn_mixer(hx, hm, ln_mix, wqkv, attn_lambda, attn_subln_g,
                                 i, j, ROW_TILE, Q_TILE, lambda_init)
            mixer_out = (a.reshape(bsz * seq, D_MODEL), a_m, ln_mix, wo)
        hx, hm = run_ffn(2 * i + 1, hx, hm, mixer_out=mixer_out)
    return hx
```

```python
import functools
import math

import jax
import jax.numpy as jnp
from jax import lax
from jax.experimental import pallas as pl
from jax.experimental.pallas import tpu as pltpu

D_MODEL = 1024
DEPTH = 4
N_MIXERS = 2
N_META = 16
D_FF = 2816
CONV_WIDTH = 3
HEAD_DIM = 64
V_DIM = 2 * HEAD_DIM
N_HEADS = D_MODEL // V_DIM
ROT_DIM = HEAD_DIM // 4
ROPE_THETA = 500000.0
NORM_EPS = 1e-6
SUBLN_EPS = 1e-5
NEG_INF = -1e30

F32 = jnp.float32
BF16 = jnp.bfloat16

V7X_VMEM_LIMIT_BYTES = 60 * 1024 * 1024
SUBLANES = 8
LANES = 128
CARRY_ROWS = SUBLANES


def _rms(t, g, eps):
    return t * lax.rsqrt(jnp.mean(t * t, axis=-1, keepdims=True) + eps) * g


def _resident(shape):
    zeros = (0,) * len(shape)
    return pl.BlockSpec(shape, lambda *_: zeros, pipeline_mode=pl.Buffered(1))


def _layer_block(shape, layer):
    zeros = (0,) * len(shape)
    return pl.BlockSpec((None,) + tuple(shape), lambda *_: (layer,) + zeros, pipeline_mode=pl.Buffered(1))


def _params(n_grid_dims):
    return pltpu.CompilerParams(
        dimension_semantics=("arbitrary",) * n_grid_dims,
        vmem_limit_bytes=V7X_VMEM_LIMIT_BYTES)


FFN_SUB_ROWS = 128
BF16_ROWS = 16


def _ffn_kernel(*refs, n_casts, mixer_out):
    hx_ref, hm_ref, g_ref, wgu_ref, wdn_ref = refs[:5]
    n_in = 5
    if mixer_out:
        ax_ref, am_ref, gmix_ref, wo_ref = refs[n_in:n_in + 4]
        n_in += 4
    w32_refs = refs[n_in:n_in + n_casts]
    ox_ref, om_ref = refs[n_in + n_casts:n_in + n_casts + 2]
    w16_refs = refs[n_in + n_casts + 2:]
    for w32_ref, w16_ref in zip(w32_refs, w16_refs):
        w16_ref[...] = w32_ref[...].astype(BF16)
    g = g_ref[...]
    sub = FFN_SUB_ROWS
    n_sub = hx_ref.shape[0] // sub

    def ffn(tiles, a_tiles):
        if mixer_out:
            ms = [jnp.dot(a, wo_ref[...], preferred_element_type=F32) for a in a_tiles]
            tiles = [h + _rms(m, gmix_ref[...][1:2], NORM_EPS) for h, m in zip(tiles, ms)]
        hns = [_rms(h, g[0:1], NORM_EPS).astype(BF16) for h in tiles]
        gus = [jnp.dot(hn, wgu_ref[...], preferred_element_type=F32) for hn in hns]
        acts = [(gu[:, :D_FF] * jax.nn.sigmoid(gu[:, :D_FF]) * gu[:, D_FF:]).astype(BF16) for gu in gus]
        fs = [jnp.dot(act, wdn_ref[...], preferred_element_type=F32) for act in acts]
        return [h + 0.5 * _rms(f, g[1:2], NORM_EPS) for h, f in zip(tiles, fs)]

    def real_tiles(x_ref):
        return [x_ref[s * sub:(s + 1) * sub] for s in range(n_sub)]

    def store_real(outs):
        for s, o in enumerate(outs):
            ox_ref[s * sub:(s + 1) * sub] = o

    @pl.when(pl.program_id(0) == 0)
    def _():
        tiles = real_tiles(hx_ref)
        tiles[0] = jnp.concatenate([hm_ref[...], tiles[0]], axis=0)
        a_tiles = None
        if mixer_out:
            a_tiles = real_tiles(ax_ref)
            a_tiles[0] = jnp.concatenate([am_ref[...], a_tiles[0]], axis=0)
        outs = ffn(tiles, a_tiles)
        om_ref[...] = outs[0][:N_META]
        outs[0] = outs[0][N_META:]
        store_real(outs)

    @pl.when(pl.program_id(0) > 0)
    def _():
        store_real(ffn(real_tiles(hx_ref), real_tiles(ax_ref) if mixer_out else None))


def _slab_rows(n_rows, steps):
    for rows in range(BF16_ROWS, n_rows + 1, BF16_ROWS):
        if n_rows % rows == 0 and n_rows // rows <= steps:
            return rows
    raise ValueError((n_rows, steps))


def _ffn(hx, hm, g, wgu, wdn, layer, tm, casts=(), mixer_out=None):
    rows = hx.shape[0]
    steps = rows // tm
    tile = pl.BlockSpec((tm, D_MODEL), lambda i: (i, 0))
    in_specs = [tile, _resident((N_META, D_MODEL)), _layer_block((2, D_MODEL), layer),
                _resident((D_MODEL, 2 * D_FF)), _resident((D_FF, D_MODEL))]
    out_specs = [tile, pl.BlockSpec((N_META, D_MODEL), lambda i: (0, 0))]
    out_shape = [jax.ShapeDtypeStruct((rows, D_MODEL), F32), jax.ShapeDtypeStruct((N_META, D_MODEL), F32)]
    args = [hx, hm, g, wgu, wdn]
    if mixer_out is not None:
        in_specs += [tile, _resident((N_META, D_MODEL)), _layer_block((2, D_MODEL), layer),
                     _resident((D_MODEL, D_MODEL))]
        args += list(mixer_out)
    for w32, index in casts:
        _, n_rows, n_cols = w32.shape
        slab = _slab_rows(n_rows, steps)
        last = n_rows // slab - 1
        in_specs.append(pl.BlockSpec((None, slab, n_cols),
                                     lambda i, index=index, last=last: (index, jnp.minimum(i, last), 0)))
        out_specs.append(pl.BlockSpec((slab, n_cols), lambda i, last=last: (jnp.minimum(i, last), 0)))
        out_shape.append(jax.ShapeDtypeStruct((n_rows, n_cols), BF16))
    return pl.pallas_call(
        functools.partial(_ffn_kernel, n_casts=len(casts), mixer_out=mixer_out is not None),
        grid=(steps,),
        in_specs=in_specs,
        out_specs=out_specs,
        out_shape=out_shape,
        compiler_params=_params(1),
        name="ffn",
    )(*args, *[c[0] for c in casts])


MIXER_SUB_ROWS = 128


def _conv_kernel(h_ref, g_ref, tail_in_ref, win_ref, cw_ref, wout_ref,
                 o_ref, tail_out_ref, carry_ref, *, tm):
    @pl.when(pl.program_id(1) == 0)
    def _():
        carry_ref[...] = tail_in_ref[...]

    g = g_ref[...]
    cw = cw_ref[...]
    n_sub = max(1, tm // MIXER_SUB_ROWS)
    sub = tm // n_sub
    hs = [h_ref[0, s * sub:(s + 1) * sub] for s in range(n_sub)]
    hns = [_rms(h, g[0:1], NORM_EPS).astype(BF16) for h in hs]
    bcxs = [jnp.dot(hn, win_ref[...], preferred_element_type=F32) for hn in hns]
    row = lax.broadcasted_iota(jnp.int32, (sub, 1), 0)
    prev = carry_ref[...]
    ys = []
    for bcx in bcxs:
        u = bcx[:, D_MODEL:2 * D_MODEL] * bcx[:, 2 * D_MODEL:]
        last = prev[CARRY_ROWS - 1:CARRY_ROWS]
        u1 = jnp.where(row == 0, last, pltpu.roll(u, 1, 0))
        u2 = jnp.where(row == 0, prev[CARRY_ROWS - 2:CARRY_ROWS - 1],
                       jnp.where(row == 1, last, pltpu.roll(u, 2, 0)))
        conv = cw[0:1] * u2 + cw[1:2] * u1 + cw[2:3] * u
        ys.append((bcx[:, :D_MODEL] * conv).astype(BF16))
        prev = u[sub - CARRY_ROWS:]
    carry_ref[...] = prev
    tail_out_ref[0] = prev
    ms = [jnp.dot(y, wout_ref[...], preferred_element_type=F32) for y in ys]
    for s in range(n_sub):
        o_ref[0, s * sub:(s + 1) * sub] = hs[s] + _rms(ms[s], g[1:2], NORM_EPS)


def _conv_mixer(h, g, tail_in, win, cw, wout, layer, mixer, tm):
    bsz, seq, _ = h.shape
    return pl.pallas_call(
        functools.partial(_conv_kernel, tm=tm),
        grid=(bsz, seq // tm),
        in_specs=[
            pl.BlockSpec((1, tm, D_MODEL), lambda b, j: (b, j, 0)),
            _layer_block((2, D_MODEL), layer),
            _resident((CARRY_ROWS, D_MODEL)),
            _resident((D_MODEL, 3 * D_MODEL)),
            _layer_block((CONV_WIDTH, D_MODEL), mixer),
            _resident((D_MODEL, D_MODEL)),
        ],
        out_specs=[
            pl.BlockSpec((1, tm, D_MODEL), lambda b, j: (b, j, 0)),
            pl.BlockSpec((1, CARRY_ROWS, D_MODEL), lambda b, j: (b, 0, 0)),
        ],
        out_shape=[
            jax.ShapeDtypeStruct((bsz, seq, D_MODEL), F32),
            jax.ShapeDtypeStruct((bsz, CARRY_ROWS, D_MODEL), F32),
        ],
        scratch_shapes=[pltpu.VMEM((CARRY_ROWS, D_MODEL), F32)],
        compiler_params=_params(2),
        name="conv_mixer",
    )(h, g, tail_in, win, cw, wout)


NT_DIMS = (((1,), (1,)), ((), ()))

ONES_ROWS = 16
VT_ROWS = V_DIM + ONES_ROWS
Q_SCALE = HEAD_DIM ** -0.5 * math.log2(math.e)


def _qkv_kernel(h_ref, g_ref, w_ref, cos_ref, sin_lo_ref, sin_hi_ref, q2_ref, k_ref, vt_ref):
    g = g_ref[...]
    hn = _rms(h_ref[0], g[0:1], NORM_EPS).astype(BF16)
    qk = jnp.dot(hn, w_ref[...], preferred_element_type=F32)
    cos = cos_ref[...]
    sin_lo = sin_lo_ref[...]
    sin_hi = sin_hi_ref[...]
    half = ROT_DIM // 2
    first_component = lax.broadcasted_iota(jnp.int32, (1, LANES), 1) < HEAD_DIM
    for blk in range(2 * N_HEADS):
        x = qk[:, blk * LANES:(blk + 1) * LANES]
        r = x * cos + pltpu.roll(x, LANES - half, 1) * sin_lo + pltpu.roll(x, half, 1) * sin_hi
        if blk < N_HEADS:
            r = (r * Q_SCALE).astype(BF16)
            zero = jnp.zeros_like(r)
            q2_ref[0, 0, :, blk * LANES:(blk + 1) * LANES] = jnp.where(first_component, r, zero)
            q2_ref[0, 1, :, blk * LANES:(blk + 1) * LANES] = jnp.where(first_component, zero, r)
        else:
            k_ref[0, :, (blk - N_HEADS) * LANES:(blk - N_HEADS + 1) * LANES] = r.astype(BF16)
    ones = jnp.ones((ONES_ROWS, qk.shape[0]), BF16)
    for head in range(N_HEADS):
        v = qk[:, 2 * D_MODEL + head * V_DIM:2 * D_MODEL + (head + 1) * V_DIM]
        vt_ref[0, head, :V_DIM, :] = v.T.astype(BF16)
        vt_ref[0, head, V_DIM:, :] = ones


def _qkv_proj(h, g, w, cos, sin_lo, sin_hi, layer, tm):
    bsz, seq, _ = h.shape
    table = pl.BlockSpec((tm, LANES), lambda b, j: (j, 0))
    return pl.pallas_call(
        _qkv_kernel,
        grid=(bsz, seq // tm),
        in_specs=[
            pl.BlockSpec((1, tm, D_MODEL), lambda b, j: (b, j, 0)),
            _layer_block((2, D_MODEL), layer),
            _resident((D_MODEL, 3 * D_MODEL)),
            table, table, table,
        ],
        out_specs=[
            pl.BlockSpec((1, 2, tm, D_MODEL), lambda b, j: (b, 0, j, 0)),
            pl.BlockSpec((1, tm, D_MODEL), lambda b, j: (b, j, 0)),
            pl.BlockSpec((1, N_HEADS, VT_ROWS, tm), lambda b, j: (b, 0, 0, j)),
        ],
        out_shape=[
            jax.ShapeDtypeStruct((bsz, 2, seq, D_MODEL), BF16),
            jax.ShapeDtypeStruct((bsz, seq, D_MODEL), BF16),
            jax.ShapeDtypeStruct((bsz, N_HEADS, VT_ROWS, seq), BF16),
        ],
        compiler_params=_params(2),
        name="qkv_proj",
    )(h, g, w, cos, sin_lo, sin_hi)


SCORE_LOOKAHEAD = 2


def _attn_t_kernel(lam_ref, sg_ref, q2_ref, k_ref, vt_ref, kp_ref, vtp_ref, o_ref, acc_ref, st_ref,
                   *, tq, n_heads, lambda_init):
    seq = k_ref.shape[1]
    ahead = min(SCORE_LOOKAHEAD, n_heads)
    key_i = lax.broadcasted_iota(jnp.int32, (tq, 2 * tq), 0)
    qry_i = lax.broadcasted_iota(jnp.int32, (tq, 2 * tq), 1)
    causal = key_i <= jnp.where(qry_i >= tq, qry_i - tq, qry_i)
    lam_p = lam_ref[...]
    lam = (jnp.exp(jnp.sum(lam_p[0:1] * lam_p[1:2], axis=1, keepdims=True))
           - jnp.exp(jnp.sum(lam_p[2:3] * lam_p[3:4], axis=1, keepdims=True)) + lambda_init)
    sg = sg_ref[...]
    heads = [slice(g * LANES, (g + 1) * LANES) for g in range(n_heads)]

    def q_tile(qi, _):
        q_start = pl.multiple_of(qi * tq, tq)
        qqs = [jnp.concatenate([q2_ref[0, 0, pl.ds(q_start, tq), hs],
                                q2_ref[0, 1, pl.ds(q_start, tq), hs]], axis=0) for hs in heads]

        def scores(g, start):
            return lax.dot_general(k_ref[0, pl.ds(start, tq), heads[g]], qqs[g], NT_DIMS,
                                   preferred_element_type=F32)

        def first_scores(g):
            prefix = lax.dot_general(kp_ref[:, heads[g]], qqs[g], NT_DIMS, preferred_element_type=F32)
            return prefix, scores(g, q_start)

        def fold(start, next_start, maxes):
            first = maxes is None
            pending = [first_scores(g) for g in range(ahead)] if first else [None] * ahead
            new_maxes = []
            for g in range(n_heads):
                st = pending.pop(0)
                if st is None:
                    st = st_ref[g]
                upcoming = g + ahead
                if upcoming < n_heads:
                    pending.append(first_scores(upcoming) if first else scores(upcoming, start))
                else:
                    st_ref[upcoming - n_heads] = scores(upcoming - n_heads, next_start)
                vt = vt_ref[0, g, :, pl.ds(start, tq)]
                if first:
                    st_p, st = st
                    st = jnp.where(causal, st, NEG_INF)
                    m_new = jnp.maximum(jnp.max(st, axis=0, keepdims=True), jnp.max(st_p, axis=0, keepdims=True))
                    acc_ref[g] = (
                        jnp.dot(vtp_ref[g], jnp.exp2(st_p - m_new).astype(BF16), preferred_element_type=F32)
                        + jnp.dot(vt, jnp.exp2(st - m_new).astype(BF16), preferred_element_type=F32))
                else:
                    m_new = jnp.maximum(maxes[g], jnp.max(st, axis=0, keepdims=True))
                    alpha = jnp.exp2(maxes[g] - m_new)
                    pv = jnp.dot(vt, jnp.exp2(st - m_new).astype(BF16), preferred_element_type=F32)
                    acc_ref[g] = alpha * acc_ref[g] + pv
                new_maxes.append(m_new)
            return tuple(new_maxes)

        def block(kb, maxes):
            start = pl.multiple_of(kb * tq, tq)
            return fold(start, start + tq, maxes)

        lax.fori_loop(0, qi, block, fold(q_start, 0, None))

        for g, hs in enumerate(heads):
            acc = acc_ref[g]
            pv = acc[:V_DIM] * (1.0 / acc[V_DIM:V_DIM + 1])
            o = pv[:, :tq] - lam * pv[:, tq:]
            o = o * lax.rsqrt(jnp.mean(o * o, axis=0, keepdims=True) + SUBLN_EPS) * sg * (1.0 - lambda_init)
            o_ref[0, pl.ds(q_start, tq), hs] = o.T.astype(BF16)
        return 0

    lax.fori_loop(0, seq // tq, q_tile, 0)


def _attention_t(q2, k, vt, lam_p, subln_g, prefix_k, prefix_vt, mixer, tq, n_heads, lambda_init):
    bsz, seq, _ = k.shape
    n_prefix = prefix_k.shape[0]
    groups = N_HEADS // n_heads
    width = n_heads * LANES
    return pl.pallas_call(
        functools.partial(_attn_t_kernel, tq=tq, n_heads=n_heads, lambda_init=lambda_init),
        grid=(bsz, groups),
        in_specs=[
            _layer_block((4, HEAD_DIM), mixer),
            _layer_block((V_DIM, 1), mixer),
            pl.BlockSpec((1, 2, seq, width), lambda b, h: (b, 0, 0, h)),
            pl.BlockSpec((1, seq, width), lambda b, h: (b, 0, h)),
            pl.BlockSpec((1, n_heads, VT_ROWS, seq), lambda b, h: (b, h, 0, 0)),
            pl.BlockSpec((n_prefix, width), lambda b, h: (0, h)),
            pl.BlockSpec((n_heads, VT_ROWS, n_prefix), lambda b, h: (h, 0, 0)),
        ],
        out_specs=pl.BlockSpec((1, seq, width), lambda b, h: (b, 0, h)),
        out_shape=jax.ShapeDtypeStruct((bsz, seq, D_MODEL), BF16),
        scratch_shapes=[pltpu.VMEM((n_heads, VT_ROWS, 2 * tq), F32),
                        pltpu.VMEM((min(SCORE_LOOKAHEAD, n_heads), tq, 2 * tq), F32)],
        compiler_params=_params(2),
        name="diff_attention",
    )(lam_p, subln_g.reshape(-1, V_DIM, 1), q2, k, vt, prefix_k, prefix_vt)


def _attn_small_kernel(lam_ref, sg_ref, q2_ref, k_ref, v_ref, o_ref, *, lambda_init):
    tq = k_ref.shape[0]
    qq = jnp.concatenate([q2_ref[0], q2_ref[1]], axis=0)
    s = lax.dot_general(qq, k_ref[...], NT_DIMS, preferred_element_type=F32)
    r = lax.broadcasted_iota(jnp.int32, (2 * tq, tq), 0)
    c = lax.broadcasted_iota(jnp.int32, (2 * tq, tq), 1)
    s = jnp.where(c <= jnp.where(r >= tq, r - tq, r), s, NEG_INF)
    p = jnp.exp2(s - jnp.max(s, axis=1, keepdims=True))
    l = jnp.sum(p, axis=1, keepdims=True)
    pv = jnp.dot(p.astype(BF16), v_ref[...], preferred_element_type=F32) / l
    lam_p = lam_ref[...]
    lam = (jnp.exp(jnp.sum(lam_p[0:1] * lam_p[1:2], axis=1, keepdims=True))
           - jnp.exp(jnp.sum(lam_p[2:3] * lam_p[3:4], axis=1, keepdims=True)) + lambda_init)
    o = pv[:tq] - lam * pv[tq:]
    o_ref[...] = (_rms(o, sg_ref[...], SUBLN_EPS) * (1.0 - lambda_init)).astype(BF16)


def _attention_small(q2, k, v, lam_p, subln_g, mixer, lambda_init):
    seq = k.shape[0]
    return pl.pallas_call(
        functools.partial(_attn_small_kernel, lambda_init=lambda_init),
        grid=(N_HEADS,),
        in_specs=[
            _layer_block((4, HEAD_DIM), mixer),
            _layer_block((1, V_DIM), mixer),
            pl.BlockSpec((2, seq, LANES), lambda h: (0, 0, h)),
            pl.BlockSpec((seq, LANES), lambda h: (0, h)),
            pl.BlockSpec((seq, LANES), lambda h: (0, h)),
        ],
        out_specs=pl.BlockSpec((seq, LANES), lambda h: (0, h)),
        out_shape=jax.ShapeDtypeStruct((seq, D_MODEL), BF16),
        compiler_params=_params(1),
        name="diff_attention_meta",
    )(lam_p, subln_g.reshape(-1, 1, V_DIM), q2, k, v)


def _rope_tables(first_pos, n_pos):
    half = ROT_DIM // 2
    pos = jnp.arange(first_pos, first_pos + n_pos, dtype=F32)
    inv_freq = ROPE_THETA ** (-jnp.arange(0, ROT_DIM, 2, dtype=F32) / ROT_DIM)
    ang = pos[:, None] * inv_freq[None, :]
    cos, sin = jnp.cos(ang), jnp.sin(ang)
    ones = jnp.ones((n_pos, HEAD_DIM - ROT_DIM), F32)
    zeros = jnp.zeros((n_pos, HEAD_DIM - half), F32)
    cos64 = jnp.concatenate([cos, cos, ones], axis=1)
    lo64 = jnp.concatenate([-sin, zeros], axis=1)
    hi64 = jnp.concatenate([jnp.zeros((n_pos, half), F32), sin, ones * 0.0], axis=1)
    return tuple(jnp.concatenate([t, t], axis=1) for t in (cos64, lo64, hi64))


def _attn_mixer(hx, hm, g, wqkv, lam_p, subln_g, layer, mixer, tm, tq, lambda_init):
    bsz, seq, _ = hx.shape
    q2_m, k_m, vt_m = (t[0] for t in _qkv_proj(hm, g, wqkv, *_rope_tables(0, N_META), layer, N_META))
    v_m = jnp.transpose(vt_m[:, :V_DIM, :], (2, 0, 1)).reshape(N_META, D_MODEL)
    a_m = _attention_small(q2_m, k_m, v_m, lam_p, subln_g, mixer, lambda_init)
    q2, k, vt = _qkv_proj(hx, g, wqkv, *_rope_tables(N_META, seq), layer, tm)
    a = _attention_t(q2, k, vt, lam_p, subln_g, k_m, vt_m, mixer, tq, HEADS_PER_STEP, lambda_init)
    return a, a_m


ROW_TILE = 512
FFN_ROW_TILE = 512
Q_TILE = 256
HEADS_PER_STEP = 8


def kernel(x, meta_tokens, ln_ffn1, ffn1_w_gu, ffn1_w_down, ln_mix, conv_w_in, conv_w, conv_w_out,
           attn_w_qkv, attn_lambda, attn_subln_g, attn_w_o, ln_ffn2, ffn2_w_gu, ffn2_w_down):
    bsz, seq, d = x.shape
    assert d == D_MODEL and seq % ROW_TILE == 0 and seq % Q_TILE == 0 and (bsz * seq) % FFN_ROW_TILE == 0
    hx = x
    hm = meta_tokens.astype(x.dtype)[None]

    ffns = [(ln, wgu, wdn, i) for i in range(DEPTH)
            for ln, wgu, wdn in ((ln_ffn1, ffn1_w_gu, ffn1_w_down), (ln_ffn2, ffn2_w_gu, ffn2_w_down))]
    ffn_w16 = [ffns[0][1][0].astype(BF16), ffns[0][2][0].astype(BF16)]
    mixer_w32 = [[(conv_w_in, i // N_MIXERS), (conv_w_out, i // N_MIXERS)] if i % N_MIXERS == 0 else
                 [(attn_w_qkv, i // N_MIXERS), (attn_w_o, i // N_MIXERS)] for i in range(DEPTH)]
    mixer_w16 = {}

    def run_ffn(n, hx, hm, mixer_out=None):
        nonlocal ffn_w16
        ln, _, _, layer = ffns[n]
        mixer_layer = 0 if n == 0 else (n + 1) // 2 if n % 2 == 1 and (n + 1) // 2 < DEPTH else None
        casts = list(mixer_w32[mixer_layer]) if mixer_layer is not None else []
        n_mixer = len(casts)
        if n + 1 < len(ffns):
            _, wgu32, wdn32, nxt = ffns[n + 1]
            casts += [(wgu32, nxt), (wdn32, nxt)]
        hx, hm, *w16 = _ffn(hx.reshape(bsz * seq, D_MODEL), hm[0], ln, *ffn_w16, layer, FFN_ROW_TILE, casts,
                            mixer_out)
        if mixer_layer is not None:
            mixer_w16[mixer_layer] = w16[:n_mixer]
        ffn_w16 = w16[n_mixer:]
        return hx.reshape(bsz, seq, D_MODEL), hm[None]

    no_history = jnp.zeros((CARRY_ROWS, D_MODEL), F32)
    for i in range(DEPTH):
        j = i // N_MIXERS
        hx, hm = run_ffn(2 * i, hx, hm)
        if i % N_MIXERS == 0:
            win, wout = mixer_w16[i]
            hm, meta_tail = _conv_mixer(hm, ln_mix, no_history, win, conv_w, wout, i, j, N_META)
            hx, _ = _conv_mixer(hx, ln_mix, meta_tail[0], win, conv_w, wout, i, j, ROW_TILE)
            mixer_out = None
        else:
            wqkv, wo = mixer_w16[i]
            lambda_init = 0.8 - 0.6 * math.exp(-0.3 * i)
            a, a_m = _attn_mixer(hx, hm, ln_mix, wqkv, attn_lambda, attn_subln_g,
                                 i, j, ROW_TILE, Q_TILE, lambda_init)
            mixer_out = (a.reshape(bsz * seq, D_MODEL), a_m, ln_mix, wo)
        hx, hm = run_ffn(2 * i + 1, hx, hm, mixer_out=mixer_out)
    return hx
```

```python
import functools
import math

import jax
import jax.numpy as jnp
from jax import lax
from jax.experimental import pallas as pl
from jax.experimental.pallas import tpu as pltpu

D_MODEL = 1024
DEPTH = 4
N_MIXERS = 2
N_META = 16
D_FF = 2816
CONV_WIDTH = 3
HEAD_DIM = 64
V_DIM = 2 * HEAD_DIM
N_HEADS = D_MODEL // V_DIM
ROT_DIM = HEAD_DIM // 4
ROPE_THETA = 500000.0
NORM_EPS = 1e-6
SUBLN_EPS = 1e-5
NEG_INF = -1e30

F32 = jnp.float32
BF16 = jnp.bfloat16

V7X_VMEM_LIMIT_BYTES = 60 * 1024 * 1024
SUBLANES = 8
LANES = 128
CARRY_ROWS = SUBLANES


def _rms(t, g, eps):
    return t * lax.rsqrt(jnp.mean(t * t, axis=-1, keepdims=True) + eps) * g


def _resident(shape):
    zeros = (0,) * len(shape)
    return pl.BlockSpec(shape, lambda *_: zeros, pipeline_mode=pl.Buffered(1))


def _layer_block(shape, layer):
    zeros = (0,) * len(shape)
    return pl.BlockSpec((None,) + tuple(shape), lambda *_: (layer,) + zeros, pipeline_mode=pl.Buffered(1))


def _params(n_grid_dims):
    return pltpu.CompilerParams(
        dimension_semantics=("arbitrary",) * n_grid_dims,
        vmem_limit_bytes=V7X_VMEM_LIMIT_BYTES)


FFN_SUB_ROWS = 128
BF16_ROWS = 16


def _ffn_kernel(*refs, n_casts, mixer_out):
    hx_ref, hm_ref, g_ref, wgu_ref, wdn_ref = refs[:5]
    n_in = 5
    if mixer_out:
        ax_ref, am_ref, gmix_ref, wo_ref = refs[n_in:n_in + 4]
        n_in += 4
    w32_refs = refs[n_in:n_in + n_casts]
    ox_ref, om_ref = refs[n_in + n_casts:n_in + n_casts + 2]
    w16_refs = refs[n_in + n_casts + 2:]
    for w32_ref, w16_ref in zip(w32_refs, w16_refs):
        w16_ref[...] = w32_ref[...].astype(BF16)
    g = g_ref[...]
    sub = FFN_SUB_ROWS
    n_sub = hx_ref.shape[0] // sub

    def ffn(tiles, a_tiles):
        if mixer_out:
            ms = [jnp.dot(a, wo_ref[...], preferred_element_type=F32) for a in a_tiles]
            tiles = [h + _rms(m, gmix_ref[...][1:2], NORM_EPS) for h, m in zip(tiles, ms)]
        hns = [_rms(h, g[0:1], NORM_EPS).astype(BF16) for h in tiles]
        gus = [jnp.dot(hn, wgu_ref[...], preferred_element_type=F32) for hn in hns]
        acts = [(gu[:, :D_FF] * jax.nn.sigmoid(gu[:, :D_FF]) * gu[:, D_FF:]).astype(BF16) for gu in gus]
        fs = [jnp.dot(act, wdn_ref[...], preferred_element_type=F32) for act in acts]
        return [h + 0.5 * _rms(f, g[1:2], NORM_EPS) for h, f in zip(tiles, fs)]

    def real_tiles(x_ref):
        return [x_ref[s * sub:(s + 1) * sub] for s in range(n_sub)]

    def store_real(outs):
        for s, o in enumerate(outs):
            ox_ref[s * sub:(s + 1) * sub] = o

    @pl.when(pl.program_id(0) == 0)
    def _():
        tiles = real_tiles(hx_ref)
        tiles[0] = jnp.concatenate([hm_ref[...], tiles[0]], axis=0)
        a_tiles = None
        if mixer_out:
            a_tiles = real_tiles(ax_ref)
            a_tiles[0] = jnp.concatenate([am_ref[...], a_tiles[0]], axis=0)
        outs = ffn(tiles, a_tiles)
        om_ref[...] = outs[0][:N_META]
        outs[0] = outs[0][N_META:]
        store_real(outs)

    @pl.when(pl.program_id(0) > 0)
    def _():
        store_real(ffn(real_tiles(hx_ref), real_tiles(ax_ref) if mixer_out else None))


def _slab_rows(n_rows, steps):
    for rows in range(BF16_ROWS, n_rows + 1, BF16_ROWS):
        if n_rows % rows == 0 and n_rows // rows <= steps:
            return rows
    raise ValueError((n_rows, steps))


def _ffn(hx, hm, g, wgu, wdn, layer, tm, casts=(), mixer_out=None):
    rows = hx.shape[0]
    steps = rows // tm
    tile = pl.BlockSpec((tm, D_MODEL), lambda i: (i, 0))
    in_specs = [tile, _resident((N_META, D_MODEL)), _layer_block((2, D_MODEL), layer),
                _resident((D_MODEL, 2 * D_FF)), _resident((D_FF, D_MODEL))]
    out_specs = [tile, pl.BlockSpec((N_META, D_MODEL), lambda i: (0, 0))]
    out_shape = [jax.ShapeDtypeStruct((rows, D_MODEL), F32), jax.ShapeDtypeStruct((N_META, D_MODEL), F32)]
    args = [hx, hm, g, wgu, wdn]
    if mixer_out is not None:
        in_specs += [tile, _resident((N_META, D_MODEL)), _layer_block((2, D_MODEL), layer),
                     _resident((D_MODEL, D_MODEL))]
        args += list(mixer_out)
    for w32, index in casts:
        _, n_rows, n_cols = w32.shape
        slab = _slab_rows(n_rows, steps)
        last = n_rows // slab - 1
        in_specs.append(pl.BlockSpec((None, slab, n_cols),
                                     lambda i, index=index, last=last: (index, jnp.minimum(i, last), 0)))
        out_specs.append(pl.BlockSpec((slab, n_cols), lambda i, last=last: (jnp.minimum(i, last), 0)))
        out_shape.append(jax.ShapeDtypeStruct((n_rows, n_cols), BF16))
    return pl.pallas_call(
        functools.partial(_ffn_kernel, n_casts=len(casts), mixer_out=mixer_out is not None),
        grid=(steps,),
        in_specs=in_specs,
        out_specs=out_specs,
        out_shape=out_shape,
        compiler_params=_params(1),
        name="ffn",
    )(*args, *[c[0] for c in casts])


MIXER_SUB_ROWS = 128


def _conv_kernel(h_ref, g_ref, tail_in_ref, win_ref, cw_ref, wout_ref,
                 o_ref, tail_out_ref, carry_ref, *, tm):
    @pl.when(pl.program_id(1) == 0)
    def _():
        carry_ref[...] = tail_in_ref[...]

    g = g_ref[...]
    cw = cw_ref[...]
    n_sub = max(1, tm // MIXER_SUB_ROWS)
    sub = tm // n_sub
    hs = [h_ref[0, s * sub:(s + 1) * sub] for s in range(n_sub)]
    hns = [_rms(h, g[0:1], NORM_EPS).astype(BF16) for h in hs]
    bcxs = [jnp.dot(hn, win_ref[...], preferred_element_type=F32) for hn in hns]
    row = lax.broadcasted_iota(jnp.int32, (sub, 1), 0)
    prev = carry_ref[...]
    ys = []
    for bcx in bcxs:
        u = bcx[:, D_MODEL:2 * D_MODEL] * bcx[:, 2 * D_MODEL:]
        last = prev[CARRY_ROWS - 1:CARRY_ROWS]
        u1 = jnp.where(row == 0, last, pltpu.roll(u, 1, 0))
        u2 = jnp.where(row == 0, prev[CARRY_ROWS - 2:CARRY_ROWS - 1],
                       jnp.where(row == 1, last, pltpu.roll(u, 2, 0)))
        conv = cw[0:1] * u2 + cw[1:2] * u1 + cw[2:3] * u
        ys.append((bcx[:, :D_MODEL] * conv).astype(BF16))
        prev = u[sub - CARRY_ROWS:]
    carry_ref[...] = prev
    tail_out_ref[0] = prev
    ms = [jnp.dot(y, wout_ref[...], preferred_element_type=F32) for y in ys]
    for s in range(n_sub):
        o_ref[0, s * sub:(s + 1) * sub] = hs[s] + _rms(ms[s], g[1:2], NORM_EPS)


def _conv_mixer(h, g, tail_in, win, cw, wout, layer, mixer, tm):
    bsz, seq, _ = h.shape
    return pl.pallas_call(
        functools.partial(_conv_kernel, tm=tm),
        grid=(bsz, seq // tm),
        in_specs=[
            pl.BlockSpec((1, tm, D_MODEL), lambda b, j: (b, j, 0)),
            _layer_block((2, D_MODEL), layer),
            _resident((CARRY_ROWS, D_MODEL)),
            _resident((D_MODEL, 3 * D_MODEL)),
            _layer_block((CONV_WIDTH, D_MODEL), mixer),
            _resident((D_MODEL, D_MODEL)),
        ],
        out_specs=[
            pl.BlockSpec((1, tm, D_MODEL), lambda b, j: (b, j, 0)),
            pl.BlockSpec((1, CARRY_ROWS, D_MODEL), lambda b, j: (b, 0, 0)),
        ],
        out_shape=[
            jax.ShapeDtypeStruct((bsz, seq, D_MODEL), F32),
            jax.ShapeDtypeStruct((bsz, CARRY_ROWS, D_MODEL), F32),
        ],
        scratch_shapes=[pltpu.VMEM((CARRY_ROWS, D_MODEL), F32)],
        compiler_params=_params(2),
        name="conv_mixer",
    )(h, g, tail_in, win, cw, wout)


NT_DIMS = (((1,), (1,)), ((), ()))

ONES_ROWS = 16
VT_ROWS = V_DIM + ONES_ROWS
Q_SCALE = HEAD_DIM ** -0.5 * math.log2(math.e)


def _qkv_kernel(h_ref, g_ref, w_ref, cos_ref, sin_lo_ref, sin_hi_ref, q2_ref, k_ref, vt_ref):
    g = g_ref[...]
    hn = _rms(h_ref[0], g[0:1], NORM_EPS).astype(BF16)
    qk = jnp.dot(hn, w_ref[...], preferred_element_type=F32)
    cos = cos_ref[...]
    sin_lo = sin_lo_ref[...]
    sin_hi = sin_hi_ref[...]
    half = ROT_DIM // 2
    first_component = lax.broadcasted_iota(jnp.int32, (1, LANES), 1) < HEAD_DIM
    for blk in range(2 * N_HEADS):
        x = qk[:, blk * LANES:(blk + 1) * LANES]
        r = x * cos + pltpu.roll(x, LANES - half, 1) * sin_lo + pltpu.roll(x, half, 1) * sin_hi
        if blk < N_HEADS:
            r = (r * Q_SCALE).astype(BF16)
            zero = jnp.zeros_like(r)
            q2_ref[0, 0, :, blk * LANES:(blk + 1) * LANES] = jnp.where(first_component, r, zero)
            q2_ref[0, 1, :, blk * LANES:(blk + 1) * LANES] = jnp.where(first_component, zero, r)
        else:
            k_ref[0, :, (blk - N_HEADS) * LANES:(blk - N_HEADS + 1) * LANES] = r.astype(BF16)
    ones = jnp.ones((ONES_ROWS, qk.shape[0]), BF16)
    for head in range(N_HEADS):
        v = qk[:, 2 * D_MODEL + head * V_DIM:2 * D_MODEL + (head + 1) * V_DIM]
        vt_ref[0, head, :V_DIM, :] = v.T.astype(BF16)
        vt_ref[0, head, V_DIM:, :] = ones


def _qkv_proj(h, g, w, cos, sin_lo, sin_hi, layer, tm):
    bsz, seq, _ = h.shape
    table = pl.BlockSpec((tm, LANES), lambda b, j: (j, 0))
    return pl.pallas_call(
        _qkv_kernel,
        grid=(bsz, seq // tm),
        in_specs=[
            pl.BlockSpec((1, tm, D_MODEL), lambda b, j: (b, j, 0)),
            _layer_block((2, D_MODEL), layer),
            _resident((D_MODEL, 3 * D_MODEL)),
            table, table, table,
        ],
        out_specs=[
            pl.BlockSpec((1, 2, tm, D_MODEL), lambda b, j: (b, 0, j, 0)),
            pl.BlockSpec((1, tm, D_MODEL), lambda b, j: (b, j, 0)),
            pl.BlockSpec((1, N_HEADS, VT_ROWS, tm), lambda b, j: (b, 0, 0, j)),
        ],
        out_shape=[
            jax.ShapeDtypeStruct((bsz, 2, seq, D_MODEL), BF16),
            jax.ShapeDtypeStruct((bsz, seq, D_MODEL), BF16),
            jax.ShapeDtypeStruct((bsz, N_HEADS, VT_ROWS, seq), BF16),
        ],
        compiler_params=_params(2),
        name="qkv_proj",
    )(h, g, w, cos, sin_lo, sin_hi)


SCORE_LOOKAHEAD = 2


def _attn_t_kernel(q2_ref, k_ref, vt_ref, kp_ref, vtp_ref, sg_ref, lam_ref, o_ref, acc_ref, st_ref,
                   *, tq, n_heads, lambda_init):
    seq = k_ref.shape[1]
    ahead = min(SCORE_LOOKAHEAD, n_heads)
    key_i = lax.broadcasted_iota(jnp.int32, (tq, 2 * tq), 0)
    qry_i = lax.broadcasted_iota(jnp.int32, (tq, 2 * tq), 1)
    causal = key_i <= jnp.where(qry_i >= tq, qry_i - tq, qry_i)
    lam_p = lam_ref[...]
    lam = (jnp.exp(jnp.sum(lam_p[0:1] * lam_p[1:2], axis=1, keepdims=True))
           - jnp.exp(jnp.sum(lam_p[2:3] * lam_p[3:4], axis=1, keepdims=True)) + lambda_init)
    sg = sg_ref[...]
    heads = [slice(g * LANES, (g + 1) * LANES) for g in range(n_heads)]

    def q_tile(qi, _):
        q_start = pl.multiple_of(qi * tq, tq)
        qqs = [jnp.concatenate([q2_ref[0, 0, pl.ds(q_start, tq), hs],
                                q2_ref[0, 1, pl.ds(q_start, tq), hs]], axis=0) for hs in heads]

        def scores(g, start):
            return lax.dot_general(k_ref[0, pl.ds(start, tq), heads[g]], qqs[g], NT_DIMS,
                                   preferred_element_type=F32)

        def first_scores(g):
            prefix = lax.dot_general(kp_ref[:, heads[g]], qqs[g], NT_DIMS, preferred_element_type=F32)
            return prefix, scores(g, q_start)

        def fold(start, next_start, maxes):
            first = maxes is None
            pending = [first_scores(g) for g in range(ahead)] if first else [None] * ahead
            new_maxes = []
            for g in range(n_heads):
                st = pending.pop(0)
                if st is None:
                    st = st_ref[g]
                upcoming = g + ahead
                if upcoming < n_heads:
                    pending.append(first_scores(upcoming) if first else scores(upcoming, start))
                else:
                    st_ref[upcoming - n_heads] = scores(upcoming - n_heads, next_start)
                vt = vt_ref[0, g, :, pl.ds(start, tq)]
                if first:
                    st_p, st = st
                    st = jnp.where(causal, st, NEG_INF)
                    m_new = jnp.maximum(jnp.max(st, axis=0, keepdims=True), jnp.max(st_p, axis=0, keepdims=True))
                    acc_ref[g] = (
                        jnp.dot(vtp_ref[g], jnp.exp2(st_p - m_new).astype(BF16), preferred_element_type=F32)
                        + jnp.dot(vt, jnp.exp2(st - m_new).astype(BF16), preferred_element_type=F32))
                else:
                    m_new = jnp.maximum(maxes[g], jnp.max(st, axis=0, keepdims=True))
                    alpha = jnp.exp2(maxes[g] - m_new)
                    pv = jnp.dot(vt, jnp.exp2(st - m_new).astype(BF16), preferred_element_type=F32)
                    acc_ref[g] = alpha * acc_ref[g] + pv
                new_maxes.append(m_new)
            return tuple(new_maxes)

        def block(kb, maxes):
            start = pl.multiple_of(kb * tq, tq)
            return fold(start, start + tq, maxes)

        lax.fori_loop(0, qi, block, fold(q_start, 0, None))

        for g, hs in enumerate(heads):
            acc = acc_ref[g]
            pv = acc[:V_DIM] * (1.0 / acc[V_DIM:V_DIM + 1])
            o = pv[:, :tq] - lam * pv[:, tq:]
            o = o * lax.rsqrt(jnp.mean(o * o, axis=0, keepdims=True) + SUBLN_EPS) * sg * (1.0 - lambda_init)
            o_ref[0, pl.ds(q_start, tq), hs] = o.T.astype(BF16)
        return 0

    lax.fori_loop(0, seq // tq, q_tile, 0)


def _attention_t(q2, k, vt, lam_p, subln_g, prefix_k, prefix_vt, mixer, tq, n_heads, lambda_init):
    bsz, seq, _ = k.shape
    n_prefix = prefix_k.shape[0]
    groups = N_HEADS // n_heads
    width = n_heads * LANES
    return pl.pallas_call(
        functools.partial(_attn_t_kernel, tq=tq, n_heads=n_heads, lambda_init=lambda_init),
        grid=(bsz, groups),
        in_specs=[
            pl.BlockSpec((1, 2, seq, width), lambda b, h: (b, 0, 0, h)),
            pl.BlockSpec((1, seq, width), lambda b, h: (b, 0, h)),
            pl.BlockSpec((1, n_heads, VT_ROWS, seq), lambda b, h: (b, h, 0, 0)),
            pl.BlockSpec((n_prefix, width), lambda b, h: (0, h)),
            pl.BlockSpec((n_heads, VT_ROWS, n_prefix), lambda b, h: (h, 0, 0)),
            _layer_block((V_DIM, 1), mixer),
            _layer_block((4, HEAD_DIM), mixer),
        ],
        out_specs=pl.BlockSpec((1, seq, width), lambda b, h: (b, 0, h)),
        out_shape=jax.ShapeDtypeStruct((bsz, seq, D_MODEL), BF16),
        scratch_shapes=[pltpu.VMEM((n_heads, VT_ROWS, 2 * tq), F32),
                        pltpu.VMEM((min(SCORE_LOOKAHEAD, n_heads), tq, 2 * tq), F32)],
        compiler_params=_params(2),
        name="diff_attention",
    )(q2, k, vt, prefix_k, prefix_vt, subln_g.reshape(-1, V_DIM, 1), lam_p)


def _attn_small_kernel(lam_ref, sg_ref, q2_ref, k_ref, v_ref, o_ref, *, lambda_init):
    tq = k_ref.shape[0]
    qq = jnp.concatenate([q2_ref[0], q2_ref[1]], axis=0)
    s = lax.dot_general(qq, k_ref[...], NT_DIMS, preferred_element_type=F32)
    r = lax.broadcasted_iota(jnp.int32, (2 * tq, tq), 0)
    c = lax.broadcasted_iota(jnp.int32, (2 * tq, tq), 1)
    s = jnp.where(c <= jnp.where(r >= tq, r - tq, r), s, NEG_INF)
    p = jnp.exp2(s - jnp.max(s, axis=1, keepdims=True))
    l = jnp.sum(p, axis=1, keepdims=True)
    pv = jnp.dot(p.astype(BF16), v_ref[...], preferred_element_type=F32) / l
    lam_p = lam_ref[...]
    lam = (jnp.exp(jnp.sum(lam_p[0:1] * lam_p[1:2], axis=1, keepdims=True))
           - jnp.exp(jnp.sum(lam_p[2:3] * lam_p[3:4], axis=1, keepdims=True)) + lambda_init)
    o = pv[:tq] - lam * pv[tq:]
    o_ref[...] = (_rms(o, sg_ref[...], SUBLN_EPS) * (1.0 - lambda_init)).astype(BF16)


def _attention_small(q2, k, v, lam_p, subln_g, mixer, lambda_init):
    seq = k.shape[0]
    return pl.pallas_call(
        functools.partial(_attn_small_kernel, lambda_init=lambda_init),
        grid=(N_HEADS,),
        in_specs=[
            _layer_block((4, HEAD_DIM), mixer),
            _layer_block((1, V_DIM), mixer),
            pl.BlockSpec((2, seq, LANES), lambda h: (0, 0, h)),
            pl.BlockSpec((seq, LANES), lambda h: (0, h)),
            pl.BlockSpec((seq, LANES), lambda h: (0, h)),
        ],
        out_specs=pl.BlockSpec((seq, LANES), lambda h: (0, h)),
        out_shape=jax.ShapeDtypeStruct((seq, D_MODEL), BF16),
        compiler_params=_params(1),
        name="diff_attention_meta",
    )(lam_p, subln_g.reshape(-1, 1, V_DIM), q2, k, v)


def _rope_tables(first_pos, n_pos):
    half = ROT_DIM // 2
    pos = jnp.arange(first_pos, first_pos + n_pos, dtype=F32)
    inv_freq = ROPE_THETA ** (-jnp.arange(0, ROT_DIM, 2, dtype=F32) / ROT_DIM)
    ang = pos[:, None] * inv_freq[None, :]
    cos, sin = jnp.cos(ang), jnp.sin(ang)
    ones = jnp.ones((n_pos, HEAD_DIM - ROT_DIM), F32)
    zeros = jnp.zeros((n_pos, HEAD_DIM - half), F32)
    cos64 = jnp.concatenate([cos, cos, ones], axis=1)
    lo64 = jnp.concatenate([-sin, zeros], axis=1)
    hi64 = jnp.concatenate([jnp.zeros((n_pos, half), F32), sin, ones * 0.0], axis=1)
    return tuple(jnp.concatenate([t, t], axis=1) for t in (cos64, lo64, hi64))


def _attn_mixer(hx, hm, g, wqkv, lam_p, subln_g, layer, mixer, tm, tq, lambda_init):
    bsz, seq, _ = hx.shape
    q2_m, k_m, vt_m = (t[0] for t in _qkv_proj(hm, g, wqkv, *_rope_tables(0, N_META), layer, N_META))
    v_m = jnp.transpose(vt_m[:, :V_DIM, :], (2, 0, 1)).reshape(N_META, D_MODEL)
    a_m = _attention_small(q2_m, k_m, v_m, lam_p, subln_g, mixer, lambda_init)
    q2, k, vt = _qkv_proj(hx, g, wqkv, *_rope_tables(N_META, seq), layer, tm)
    a = _attention_t(q2, k, vt, lam_p, subln_g, k_m, vt_m, mixer, tq, HEADS_PER_STEP, lambda_init)
    return a, a_m


ROW_TILE = 512
FFN_ROW_TILE = 512
Q_TILE = 256
HEADS_PER_STEP = 8


def kernel(x, meta_tokens, ln_ffn1, ffn1_w_gu, ffn1_w_down, ln_mix, conv_w_in, conv_w, conv_w_out,
           attn_w_qkv, attn_lambda, attn_subln_g, attn_w_o, ln_ffn2, ffn2_w_gu, ffn2_w_down):
    bsz, seq, d = x.shape
    assert d == D_MODEL and seq % ROW_TILE == 0 and seq % Q_TILE == 0 and (bsz * seq) % FFN_ROW_TILE == 0
    hx = x
    hm = meta_tokens.astype(x.dtype)[None]

    ffns = [(ln, wgu, wdn, i) for i in range(DEPTH)
            for ln, wgu, wdn in ((ln_ffn1, ffn1_w_gu, ffn1_w_down), (ln_ffn2, ffn2_w_gu, ffn2_w_down))]
    ffn_w16 = [ffns[0][1][0].astype(BF16), ffns[0][2][0].astype(BF16)]
    mixer_w32 = [[(conv_w_in, i // N_MIXERS), (conv_w_out, i // N_MIXERS)] if i % N_MIXERS == 0 else
                 [(attn_w_qkv, i // N_MIXERS), (attn_w_o, i // N_MIXERS)] for i in range(DEPTH)]
    mixer_w16 = {}

    def run_ffn(n, hx, hm, mixer_out=None):
        nonlocal ffn_w16
        ln, _, _, layer = ffns[n]
        mixer_layer = 0 if n == 0 else (n + 1) // 2 if n % 2 == 1 and (n + 1) // 2 < DEPTH else None
        casts = list(mixer_w32[mixer_layer]) if mixer_layer is not None else []
        n_mixer = len(casts)
        if n + 1 < len(ffns):
            _, wgu32, wdn32, nxt = ffns[n + 1]
            casts += [(wgu32, nxt), (wdn32, nxt)]
        hx, hm, *w16 = _ffn(hx.reshape(bsz * seq, D_MODEL), hm[0], ln, *ffn_w16, layer, FFN_ROW_TILE, casts,
                            mixer_out)
        if mixer_layer is not None:
            mixer_w16[mixer_layer] = w16[:n_mixer]
        ffn_w16 = w16[n_mixer:]
        return hx.reshape(bsz, seq, D_MODEL), hm[None]

    no_history = jnp.zeros((CARRY_ROWS, D_MODEL), F32)
    for i in range(DEPTH):
        j = i // N_MIXERS
        hx, hm = run_ffn(2 * i, hx, hm)
        if i % N_MIXERS == 0:
            win, wout = mixer_w16[i]
            hm, meta_tail = _conv_mixer(hm, ln_mix, no_history, win, conv_w, wout, i, j, N_META)
            hx, _ = _conv_mixer(hx, ln_mix, meta_tail[0], win, conv_w, wout, i, j, ROW_TILE)
            mixer_out = None
        else:
            wqkv, wo = mixer_w16[i]
            lambda_init = 0.8 - 0.6 * math.exp(-0.3 * i)
            a, a_m = _attn_mixer(hx, hm, ln_mix, wqkv, attn_lambda, attn_subln_g,
                                 i, j, ROW_TILE, Q_TILE, lambda_init)
            mixer_out = (a.reshape(bsz * seq, D_MODEL), a_m, ln_mix, wo)
        hx, hm = run_ffn(2 * i + 1, hx, hm, mixer_out=mixer_out)
    return hx
```

```python
import functools
import math

import jax
import jax.numpy as jnp
from jax import lax
from jax.experimental import pallas as pl
from jax.experimental.pallas import tpu as pltpu

D_MODEL = 1024
DEPTH = 4
N_MIXERS = 2
N_META = 16
D_FF = 2816
CONV_WIDTH = 3
HEAD_DIM = 64
V_DIM = 2 * HEAD_DIM
N_HEADS = D_MODEL // V_DIM
ROT_DIM = HEAD_DIM // 4
ROPE_THETA = 500000.0
NORM_EPS = 1e-6
SUBLN_EPS = 1e-5
NEG_INF = -1e30

F32 = jnp.float32
BF16 = jnp.bfloat16

V7X_VMEM_LIMIT_BYTES = 60 * 1024 * 1024
SUBLANES = 8
LANES = 128
CARRY_ROWS = SUBLANES


def _rms(t, g, eps):
    return t * lax.rsqrt(jnp.mean(t * t, axis=-1, keepdims=True) + eps) * g


def _resident(shape):
    zeros = (0,) * len(shape)
    return pl.BlockSpec(shape, lambda *_: zeros, pipeline_mode=pl.Buffered(1))


def _layer_block(shape, layer):
    zeros = (0,) * len(shape)
    return pl.BlockSpec((None,) + tuple(shape), lambda *_: (layer,) + zeros, pipeline_mode=pl.Buffered(1))


def _params(n_grid_dims):
    return pltpu.CompilerParams(
        dimension_semantics=("arbitrary",) * n_grid_dims,
        vmem_limit_bytes=V7X_VMEM_LIMIT_BYTES)


FFN_SUB_ROWS = 128
BF16_ROWS = 16


def _ffn_kernel(*refs, n_casts, mixer_out):
    hx_ref, hm_ref, g_ref, wgu_ref, wdn_ref = refs[:5]
    n_in = 5
    if mixer_out:
        ax_ref, am_ref, gmix_ref, wo_ref = refs[n_in:n_in + 4]
        n_in += 4
    w32_refs = refs[n_in:n_in + n_casts]
    ox_ref, om_ref = refs[n_in + n_casts:n_in + n_casts + 2]
    w16_refs = refs[n_in + n_casts + 2:]
    for w32_ref, w16_ref in zip(w32_refs, w16_refs):
        w16_ref[...] = w32_ref[...].astype(BF16)
    g = g_ref[...]
    sub = FFN_SUB_ROWS
    n_sub = hx_ref.shape[0] // sub

    def ffn(tiles, a_tiles):
        if mixer_out:
            ms = [jnp.dot(a, wo_ref[...], preferred_element_type=F32) for a in a_tiles]
            tiles = [h + _rms(m, gmix_ref[...][1:2], NORM_EPS) for h, m in zip(tiles, ms)]
        hns = [_rms(h, g[0:1], NORM_EPS).astype(BF16) for h in tiles]
        gus = [jnp.dot(hn, wgu_ref[...], preferred_element_type=F32) for hn in hns]
        acts = [(gu[:, :D_FF] * jax.nn.sigmoid(gu[:, :D_FF]) * gu[:, D_FF:]).astype(BF16) for gu in gus]
        fs = [jnp.dot(act, wdn_ref[...], preferred_element_type=F32) for act in acts]
        return [h + 0.5 * _rms(f, g[1:2], NORM_EPS) for h, f in zip(tiles, fs)]

    def real_tiles(x_ref):
        return [x_ref[s * sub:(s + 1) * sub] for s in range(n_sub)]

    def store_real(outs):
        for s, o in enumerate(outs):
            ox_ref[s * sub:(s + 1) * sub] = o

    @pl.when(pl.program_id(0) == 0)
    def _():
        tiles = real_tiles(hx_ref)
        tiles[0] = jnp.concatenate([hm_ref[...], tiles[0]], axis=0)
        a_tiles = None
        if mixer_out:
            a_tiles = real_tiles(ax_ref)
            a_tiles[0] = jnp.concatenate([am_ref[...], a_tiles[0]], axis=0)
        outs = ffn(tiles, a_tiles)
        om_ref[...] = outs[0][:N_META]
        outs[0] = outs[0][N_META:]
        store_real(outs)

    @pl.when(pl.program_id(0) > 0)
    def _():
        store_real(ffn(real_tiles(hx_ref), real_tiles(ax_ref) if mixer_out else None))


def _slab_rows(n_rows, steps):
    for rows in range(BF16_ROWS, n_rows + 1, BF16_ROWS):
        if n_rows % rows == 0 and n_rows // rows <= steps:
            return rows
    raise ValueError((n_rows, steps))


def _ffn(hx, hm, g, wgu, wdn, layer, tm, casts=(), mixer_out=None):
    rows = hx.shape[0]
    steps = rows // tm
    tile = pl.BlockSpec((tm, D_MODEL), lambda i: (i, 0))
    in_specs = [tile, _resident((N_META, D_MODEL)), _layer_block((2, D_MODEL), layer),
                _resident((D_MODEL, 2 * D_FF)), _resident((D_FF, D_MODEL))]
    out_specs = [tile, pl.BlockSpec((N_META, D_MODEL), lambda i: (0, 0))]
    out_shape = [jax.ShapeDtypeStruct((rows, D_MODEL), F32), jax.ShapeDtypeStruct((N_META, D_MODEL), F32)]
    args = [hx, hm, g, wgu, wdn]
    if mixer_out is not None:
        in_specs += [tile, _resident((N_META, D_MODEL)), _layer_block((2, D_MODEL), layer),
                     _resident((D_MODEL, D_MODEL))]
        args += list(mixer_out)
    for w32, index in casts:
        _, n_rows, n_cols = w32.shape
        slab = _slab_rows(n_rows, steps)
        last = n_rows // slab - 1
        in_specs.append(pl.BlockSpec((None, slab, n_cols),
                                     lambda i, index=index, last=last: (index, jnp.minimum(i, last), 0)))
        out_specs.append(pl.BlockSpec((slab, n_cols), lambda i, last=last: (jnp.minimum(i, last), 0)))
        out_shape.append(jax.ShapeDtypeStruct((n_rows, n_cols), BF16))
    return pl.pallas_call(
        functools.partial(_ffn_kernel, n_casts=len(casts), mixer_out=mixer_out is not None),
        grid=(steps,),
        in_specs=in_specs,
        out_specs=out_specs,
        out_shape=out_shape,
        compiler_params=_params(1),
        name="ffn",
    )(*args, *[c[0] for c in casts])


MIXER_SUB_ROWS = 128


def _conv_kernel(h_ref, g_ref, tail_in_ref, win_ref, cw_ref, wout_ref,
                 o_ref, tail_out_ref, carry_ref, *, tm):
    @pl.when(pl.program_id(1) == 0)
    def _():
        carry_ref[...] = tail_in_ref[...]

    g = g_ref[...]
    cw = cw_ref[...]
    n_sub = max(1, tm // MIXER_SUB_ROWS)
    sub = tm // n_sub
    hs = [h_ref[0, s * sub:(s + 1) * sub] for s in range(n_sub)]
    hns = [_rms(h, g[0:1], NORM_EPS).astype(BF16) for h in hs]
    bcxs = [jnp.dot(hn, win_ref[...], preferred_element_type=F32) for hn in hns]
    row = lax.broadcasted_iota(jnp.int32, (sub, 1), 0)
    prev = carry_ref[...]
    ys = []
    for bcx in bcxs:
        u = bcx[:, D_MODEL:2 * D_MODEL] * bcx[:, 2 * D_MODEL:]
        last = prev[CARRY_ROWS - 1:CARRY_ROWS]
        u1 = jnp.where(row == 0, last, pltpu.roll(u, 1, 0))
        u2 = jnp.where(row == 0, prev[CARRY_ROWS - 2:CARRY_ROWS - 1],
                       jnp.where(row == 1, last, pltpu.roll(u, 2, 0)))
        conv = cw[0:1] * u2 + cw[1:2] * u1 + cw[2:3] * u
        ys.append((bcx[:, :D_MODEL] * conv).astype(BF16))
        prev = u[sub - CARRY_ROWS:]
    carry_ref[...] = prev
    tail_out_ref[0] = prev
    ms = [jnp.dot(y, wout_ref[...], preferred_element_type=F32) for y in ys]
    for s in range(n_sub):
        o_ref[0, s * sub:(s + 1) * sub] = hs[s] + _rms(ms[s], g[1:2], NORM_EPS)


def _conv_mixer(h, g, tail_in, win, cw, wout, layer, mixer, tm):
    bsz, seq, _ = h.shape
    return pl.pallas_call(
        functools.partial(_conv_kernel, tm=tm),
        grid=(bsz, seq // tm),
        in_specs=[
            pl.BlockSpec((1, tm, D_MODEL), lambda b, j: (b, j, 0)),
            _layer_block((2, D_MODEL), layer),
            _resident((CARRY_ROWS, D_MODEL)),
            _resident((D_MODEL, 3 * D_MODEL)),
            _layer_block((CONV_WIDTH, D_MODEL), mixer),
            _resident((D_MODEL, D_MODEL)),
        ],
        out_specs=[
            pl.BlockSpec((1, tm, D_MODEL), lambda b, j: (b, j, 0)),
            pl.BlockSpec((1, CARRY_ROWS, D_MODEL), lambda b, j: (b, 0, 0)),
        ],
        out_shape=[
            jax.ShapeDtypeStruct((bsz, seq, D_MODEL), F32),
            jax.ShapeDtypeStruct((bsz, CARRY_ROWS, D_MODEL), F32),
        ],
        scratch_shapes=[pltpu.VMEM((CARRY_ROWS, D_MODEL), F32)],
        compiler_params=_params(2),
        name="conv_mixer",
    )(h, g, tail_in, win, cw, wout)


NT_DIMS = (((1,), (1,)), ((), ()))

ONES_ROWS = 16
VT_ROWS = V_DIM + ONES_ROWS
Q_SCALE = HEAD_DIM ** -0.5 * math.log2(math.e)


def _qkv_kernel(h_ref, g_ref, w_ref, cos_ref, sin_lo_ref, sin_hi_ref, q2_ref, k_ref, vt_ref):
    g = g_ref[...]
    hn = _rms(h_ref[0], g[0:1], NORM_EPS).astype(BF16)
    qk = jnp.dot(hn, w_ref[...], preferred_element_type=F32)
    cos = cos_ref[...]
    sin_lo = sin_lo_ref[...]
    sin_hi = sin_hi_ref[...]
    half = ROT_DIM // 2
    first_component = lax.broadcasted_iota(jnp.int32, (1, LANES), 1) < HEAD_DIM
    for blk in range(2 * N_HEADS):
        x = qk[:, blk * LANES:(blk + 1) * LANES]
        r = x * cos + pltpu.roll(x, LANES - half, 1) * sin_lo + pltpu.roll(x, half, 1) * sin_hi
        if blk < N_HEADS:
            r = (r * Q_SCALE).astype(BF16)
            zero = jnp.zeros_like(r)
            q2_ref[0, 0, :, blk * LANES:(blk + 1) * LANES] = jnp.where(first_component, r, zero)
            q2_ref[0, 1, :, blk * LANES:(blk + 1) * LANES] = jnp.where(first_component, zero, r)
        else:
            k_ref[0, :, (blk - N_HEADS) * LANES:(blk - N_HEADS + 1) * LANES] = r.astype(BF16)
    ones = jnp.ones((ONES_ROWS, qk.shape[0]), BF16)
    for head in range(N_HEADS):
        v = qk[:, 2 * D_MODEL + head * V_DIM:2 * D_MODEL + (head + 1) * V_DIM]
        vt_ref[0, head, :V_DIM, :] = v.T.astype(BF16)
        vt_ref[0, head, V_DIM:, :] = ones


def _qkv_proj(h, g, w, cos, sin_lo, sin_hi, layer, tm):
    bsz, seq, _ = h.shape
    table = pl.BlockSpec((tm, LANES), lambda b, j: (j, 0))
    return pl.pallas_call(
        _qkv_kernel,
        grid=(bsz, seq // tm),
        in_specs=[
            pl.BlockSpec((1, tm, D_MODEL), lambda b, j: (b, j, 0)),
            _layer_block((2, D_MODEL), layer),
            _resident((D_MODEL, 3 * D_MODEL)),
            table, table, table,
        ],
        out_specs=[
            pl.BlockSpec((1, 2, tm, D_MODEL), lambda b, j: (b, 0, j, 0)),
            pl.BlockSpec((1, tm, D_MODEL), lambda b, j: (b, j, 0)),
            pl.BlockSpec((1, N_HEADS, VT_ROWS, tm), lambda b, j: (b, 0, 0, j)),
        ],
        out_shape=[
            jax.ShapeDtypeStruct((bsz, 2, seq, D_MODEL), BF16),
            jax.ShapeDtypeStruct((bsz, seq, D_MODEL), BF16),
            jax.ShapeDtypeStruct((bsz, N_HEADS, VT_ROWS, seq), BF16),
        ],
        compiler_params=_params(2),
        name="qkv_proj",
    )(h, g, w, cos, sin_lo, sin_hi)


SCORE_LOOKAHEAD = 2


def _attn_t_kernel(q2_ref, k_ref, vt_ref, kp_ref, vtp_ref, sg_ref, lam_ref, o_ref, acc_ref, st_ref,
                   *, tq, n_heads, lambda_init):
    seq = k_ref.shape[1]
    ahead = min(SCORE_LOOKAHEAD, n_heads)
    key_i = lax.broadcasted_iota(jnp.int32, (tq, 2 * tq), 0)
    qry_i = lax.broadcasted_iota(jnp.int32, (tq, 2 * tq), 1)
    causal = key_i <= jnp.where(qry_i >= tq, qry_i - tq, qry_i)
    lam_p = lam_ref[...]
    lam = (jnp.exp(jnp.sum(lam_p[0:1] * lam_p[1:2], axis=1, keepdims=True))
           - jnp.exp(jnp.sum(lam_p[2:3] * lam_p[3:4], axis=1, keepdims=True)) + lambda_init)
    sg = sg_ref[...]
    heads = [slice(g * LANES, (g + 1) * LANES) for g in range(n_heads)]

    def q_tile(qi, _):
        q_start = pl.multiple_of(qi * tq, tq)
        qqs = [jnp.concatenate([q2_ref[0, 0, pl.ds(q_start, tq), hs],
                                q2_ref[0, 1, pl.ds(q_start, tq), hs]], axis=0) for hs in heads]

        def scores(g, start):
            return lax.dot_general(k_ref[0, pl.ds(start, tq), heads[g]], qqs[g], NT_DIMS,
                                   preferred_element_type=F32)

        def first_scores(g):
            k_all = jnp.concatenate([k_ref[0, pl.ds(q_start, tq), heads[g]], kp_ref[:, heads[g]]], axis=0)
            st_all = lax.dot_general(k_all, qqs[g], NT_DIMS, preferred_element_type=F32)
            return st_all[tq:], st_all[:tq]

        def fold(start, next_start, maxes):
            first = maxes is None
            pending = [first_scores(g) for g in range(ahead)] if first else [None] * ahead
            new_maxes = []
            for g in range(n_heads):
                st = pending.pop(0)
                if st is None:
                    st = st_ref[g]
                upcoming = g + ahead
                if upcoming < n_heads:
                    pending.append(first_scores(upcoming) if first else scores(upcoming, start))
                else:
                    st_ref[upcoming - n_heads] = scores(upcoming - n_heads, next_start)
                vt = vt_ref[0, g, :, pl.ds(start, tq)]
                if first:
                    st_p, st = st
                    st = jnp.where(causal, st, NEG_INF)
                    m_new = jnp.maximum(jnp.max(st, axis=0, keepdims=True), jnp.max(st_p, axis=0, keepdims=True))
                    p_all = jnp.concatenate([jnp.exp2(st - m_new).astype(BF16),
                                             jnp.exp2(st_p - m_new).astype(BF16)], axis=0)
                    vt_all = jnp.concatenate([vt, vtp_ref[g]], axis=1)
                    acc_ref[g] = jnp.dot(vt_all, p_all, preferred_element_type=F32)
                else:
                    m_new = jnp.maximum(maxes[g], jnp.max(st, axis=0, keepdims=True))
                    alpha = jnp.exp2(maxes[g] - m_new)
                    pv = jnp.dot(vt, jnp.exp2(st - m_new).astype(BF16), preferred_element_type=F32)
                    acc_ref[g] = alpha * acc_ref[g] + pv
                new_maxes.append(m_new)
            return tuple(new_maxes)

        def block(kb, maxes):
            start = pl.multiple_of(kb * tq, tq)
            return fold(start, start + tq, maxes)

        lax.fori_loop(0, qi, block, fold(q_start, 0, None))

        for g, hs in enumerate(heads):
            acc = acc_ref[g]
            pv = acc[:V_DIM] * (1.0 / acc[V_DIM:V_DIM + 1])
            o = pv[:, :tq] - lam * pv[:, tq:]
            o = o * lax.rsqrt(jnp.mean(o * o, axis=0, keepdims=True) + SUBLN_EPS) * sg * (1.0 - lambda_init)
            o_ref[0, pl.ds(q_start, tq), hs] = o.T.astype(BF16)
        return 0

    lax.fori_loop(0, seq // tq, q_tile, 0)


def _attention_t(q2, k, vt, lam_p, subln_g, prefix_k, prefix_vt, mixer, tq, n_heads, lambda_init):
    bsz, seq, _ = k.shape
    n_prefix = prefix_k.shape[0]
    groups = N_HEADS // n_heads
    width = n_heads * LANES
    return pl.pallas_call(
        functools.partial(_attn_t_kernel, tq=tq, n_heads=n_heads, lambda_init=lambda_init),
        grid=(bsz, groups),
        in_specs=[
            pl.BlockSpec((1, 2, seq, width), lambda b, h: (b, 0, 0, h)),
            pl.BlockSpec((1, seq, width), lambda b, h: (b, 0, h)),
            pl.BlockSpec((1, n_heads, VT_ROWS, seq), lambda b, h: (b, h, 0, 0)),
            pl.BlockSpec((n_prefix, width), lambda b, h: (0, h)),
            pl.BlockSpec((n_heads, VT_ROWS, n_prefix), lambda b, h: (h, 0, 0)),
            _layer_block((V_DIM, 1), mixer),
            _layer_block((4, HEAD_DIM), mixer),
        ],
        out_specs=pl.BlockSpec((1, seq, width), lambda b, h: (b, 0, h)),
        out_shape=jax.ShapeDtypeStruct((bsz, seq, D_MODEL), BF16),
        scratch_shapes=[pltpu.VMEM((n_heads, VT_ROWS, 2 * tq), F32),
                        pltpu.VMEM((min(SCORE_LOOKAHEAD, n_heads), tq, 2 * tq), F32)],
        compiler_params=_params(2),
        name="diff_attention",
    )(q2, k, vt, prefix_k, prefix_vt, subln_g.reshape(-1, V_DIM, 1), lam_p)


def _attn_small_kernel(lam_ref, sg_ref, q2_ref, k_ref, v_ref, o_ref, *, lambda_init):
    tq = k_ref.shape[0]
    qq = jnp.concatenate([q2_ref[0], q2_ref[1]], axis=0)
    s = lax.dot_general(qq, k_ref[...], NT_DIMS, preferred_element_type=F32)
    r = lax.broadcasted_iota(jnp.int32, (2 * tq, tq), 0)
    c = lax.broadcasted_iota(jnp.int32, (2 * tq, tq), 1)
    s = jnp.where(c <= jnp.where(r >= tq, r - tq, r), s, NEG_INF)
    p = jnp.exp2(s - jnp.max(s, axis=1, keepdims=True))
    l = jnp.sum(p, axis=1, keepdims=True)
    pv = jnp.dot(p.astype(BF16), v_ref[...], preferred_element_type=F32) / l
    lam_p = lam_ref[...]
    lam = (jnp.exp(jnp.sum(lam_p[0:1] * lam_p[1:2], axis=1, keepdims=True))
           - jnp.exp(jnp.sum(lam_p[2:3] * lam_p[3:4], axis=1, keepdims=True)) + lambda_init)
    o = pv[:tq] - lam * pv[tq:]
    o_ref[...] = (_rms(o, sg_ref[...], SUBLN_EPS) * (1.0 - lambda_init)).astype(BF16)


def _attention_small(q2, k, v, lam_p, subln_g, mixer, lambda_init):
    seq = k.shape[0]
    return pl.pallas_call(
        functools.partial(_attn_small_kernel, lambda_init=lambda_init),
        grid=(N_HEADS,),
        in_specs=[
            _layer_block((4, HEAD_DIM), mixer),
            _layer_block((1, V_DIM), mixer),
            pl.BlockSpec((2, seq, LANES), lambda h: (0, 0, h)),
            pl.BlockSpec((seq, LANES), lambda h: (0, h)),
            pl.BlockSpec((seq, LANES), lambda h: (0, h)),
        ],
        out_specs=pl.BlockSpec((seq, LANES), lambda h: (0, h)),
        out_shape=jax.ShapeDtypeStruct((seq, D_MODEL), BF16),
        compiler_params=_params(1),
        name="diff_attention_meta",
    )(lam_p, subln_g.reshape(-1, 1, V_DIM), q2, k, v)


def _rope_tables(first_pos, n_pos):
    half = ROT_DIM // 2
    pos = jnp.arange(first_pos, first_pos + n_pos, dtype=F32)
    inv_freq = ROPE_THETA ** (-jnp.arange(0, ROT_DIM, 2, dtype=F32) / ROT_DIM)
    ang = pos[:, None] * inv_freq[None, :]
    cos, sin = jnp.cos(ang), jnp.sin(ang)
    ones = jnp.ones((n_pos, HEAD_DIM - ROT_DIM), F32)
    zeros = jnp.zeros((n_pos, HEAD_DIM - half), F32)
    cos64 = jnp.concatenate([cos, cos, ones], axis=1)
    lo64 = jnp.concatenate([-sin, zeros], axis=1)
    hi64 = jnp.concatenate([jnp.zeros((n_pos, half), F32), sin, ones * 0.0], axis=1)
    return tuple(jnp.concatenate([t, t], axis=1) for t in (cos64, lo64, hi64))


def _attn_mixer(hx, hm, g, wqkv, lam_p, subln_g, layer, mixer, tm, tq, lambda_init):
    bsz, seq, _ = hx.shape
    q2_m, k_m, vt_m = (t[0] for t in _qkv_proj(hm, g, wqkv, *_rope_tables(0, N_META), layer, N_META))
    v_m = jnp.transpose(vt_m[:, :V_DIM, :], (2, 0, 1)).reshape(N_META, D_MODEL)
    a_m = _attention_small(q2_m, k_m, v_m, lam_p, subln_g, mixer, lambda_init)
    q2, k, vt = _qkv_proj(hx, g, wqkv, *_rope_tables(N_META, seq), layer, tm)
    a = _attention_t(q2, k, vt, lam_p, subln_g, k_m, vt_m, mixer, tq, HEADS_PER_STEP, lambda_init)
    return a, a_m


ROW_TILE = 512
FFN_ROW_TILE = 512
Q_TILE = 256
HEADS_PER_STEP = 8


def kernel(x, meta_tokens, ln_ffn1, ffn1_w_gu, ffn1_w_down, ln_mix, conv_w_in, conv_w, conv_w_out,
           attn_w_qkv, attn_lambda, attn_subln_g, attn_w_o, ln_ffn2, ffn2_w_gu, ffn2_w_down):
    bsz, seq, d = x.shape
    assert d == D_MODEL and seq % ROW_TILE == 0 and seq % Q_TILE == 0 and (bsz * seq) % FFN_ROW_TILE == 0
    hx = x
    hm = meta_tokens.astype(x.dtype)[None]

    ffns = [(ln, wgu, wdn, i) for i in range(DEPTH)
            for ln, wgu, wdn in ((ln_ffn1, ffn1_w_gu, ffn1_w_down), (ln_ffn2, ffn2_w_gu, ffn2_w_down))]
    ffn_w16 = [ffns[0][1][0].astype(BF16), ffns[0][2][0].astype(BF16)]
    mixer_w32 = [[(conv_w_in, i // N_MIXERS), (conv_w_out, i // N_MIXERS)] if i % N_MIXERS == 0 else
                 [(attn_w_qkv, i // N_MIXERS), (attn_w_o, i // N_MIXERS)] for i in range(DEPTH)]
    mixer_w16 = {}

    def run_ffn(n, hx, hm, mixer_out=None):
        nonlocal ffn_w16
        ln, _, _, layer = ffns[n]
        mixer_layer = 0 if n == 0 else (n + 1) // 2 if n % 2 == 1 and (n + 1) // 2 < DEPTH else None
        casts = list(mixer_w32[mixer_layer]) if mixer_layer is not None else []
        n_mixer = len(casts)
        if n + 1 < len(ffns):
            _, wgu32, wdn32, nxt = ffns[n + 1]
            casts += [(wgu32, nxt), (wdn32, nxt)]
        hx, hm, *w16 = _ffn(hx.reshape(bsz * seq, D_MODEL), hm[0], ln, *ffn_w16, layer, FFN_ROW_TILE, casts,
                            mixer_out)
        if mixer_layer is not None:
            mixer_w16[mixer_layer] = w16[:n_mixer]
        ffn_w16 = w16[n_mixer:]
        return hx.reshape(bsz, seq, D_MODEL), hm[None]

    no_history = jnp.zeros((CARRY_ROWS, D_MODEL), F32)
    for i in range(DEPTH):
        j = i // N_MIXERS
        hx, hm = run_ffn(2 * i, hx, hm)
        if i % N_MIXERS == 0:
            win, wout = mixer_w16[i]
            hm, meta_tail = _conv_mixer(hm, ln_mix, no_history, win, conv_w, wout, i, j, N_META)
            hx, _ = _conv_mixer(hx, ln_mix, meta_tail[0], win, conv_w, wout, i, j, ROW_TILE)
            mixer_out = None
        else:
            wqkv, wo = mixer_w16[i]
            lambda_init = 0.8 - 0.6 * math.exp(-0.3 * i)
            a, a_m = _attn_mixer(hx, hm, ln_mix, wqkv, attn_lambda, attn_subln_g,
                                 i, j, ROW_TILE, Q_TILE, lambda_init)
            mixer_out = (a.reshape(bsz * seq, D_MODEL), a_m, ln_mix, wo)
        hx, hm = run_ffn(2 * i + 1, hx, hm, mixer_out=mixer_out)
    return hx
```

```python
import functools
import math

import jax
import jax.numpy as jnp
from jax import lax
from jax.experimental import pallas as pl
from jax.experimental.pallas import tpu as pltpu

D_MODEL = 1024
DEPTH = 4
N_MIXERS = 2
N_META = 16
D_FF = 2816
CONV_WIDTH = 3
HEAD_DIM = 64
V_DIM = 2 * HEAD_DIM
N_HEADS = D_MODEL // V_DIM
ROT_DIM = HEAD_DIM // 4
ROPE_THETA = 500000.0
NORM_EPS = 1e-6
SUBLN_EPS = 1e-5
NEG_INF = -1e30

F32 = jnp.float32
BF16 = jnp.bfloat16

V7X_VMEM_LIMIT_BYTES = 60 * 1024 * 1024
SUBLANES = 8
LANES = 128
CARRY_ROWS = SUBLANES


def _rms(t, g, eps):
    return t * lax.rsqrt(jnp.mean(t * t, axis=-1, keepdims=True) + eps) * g


def _resident(shape):
    zeros = (0,) * len(shape)
    return pl.BlockSpec(shape, lambda *_: zeros, pipeline_mode=pl.Buffered(1))


def _layer_block(shape, layer):
    zeros = (0,) * len(shape)
    return pl.BlockSpec((None,) + tuple(shape), lambda *_: (layer,) + zeros, pipeline_mode=pl.Buffered(1))


def _params(n_grid_dims):
    return pltpu.CompilerParams(
        dimension_semantics=("arbitrary",) * n_grid_dims,
        vmem_limit_bytes=V7X_VMEM_LIMIT_BYTES)


FFN_SUB_ROWS = 128
BF16_ROWS = 16


def _ffn_kernel(*refs, n_casts, mixer_out):
    hx_ref, hm_ref, g_ref, wgu_ref, wdn_ref = refs[:5]
    n_in = 5
    if mixer_out:
        ax_ref, am_ref, gmix_ref, wo_ref = refs[n_in:n_in + 4]
        n_in += 4
    w32_refs = refs[n_in:n_in + n_casts]
    ox_ref, om_ref = refs[n_in + n_casts:n_in + n_casts + 2]
    w16_refs = refs[n_in + n_casts + 2:]
    for w32_ref, w16_ref in zip(w32_refs, w16_refs):
        w16_ref[...] = w32_ref[...].astype(BF16)
    g = g_ref[...]
    sub = FFN_SUB_ROWS
    n_sub = hx_ref.shape[0] // sub

    def ffn(tiles, a_tiles):
        if mixer_out:
            ms = [jnp.dot(a, wo_ref[...], preferred_element_type=F32) for a in a_tiles]
            tiles = [h + _rms(m, gmix_ref[...][1:2], NORM_EPS) for h, m in zip(tiles, ms)]
        hns = [_rms(h, g[0:1], NORM_EPS).astype(BF16) for h in tiles]
        gus = [jnp.dot(hn, wgu_ref[...], preferred_element_type=F32) for hn in hns]
        acts = [(gu[:, :D_FF] * jax.nn.sigmoid(gu[:, :D_FF]) * gu[:, D_FF:]).astype(BF16) for gu in gus]
        fs = [jnp.dot(act, wdn_ref[...], preferred_element_type=F32) for act in acts]
        return [h + 0.5 * _rms(f, g[1:2], NORM_EPS) for h, f in zip(tiles, fs)]

    def real_tiles(x_ref):
        return [x_ref[s * sub:(s + 1) * sub] for s in range(n_sub)]

    def store_real(outs):
        for s, o in enumerate(outs):
            ox_ref[s * sub:(s + 1) * sub] = o

    @pl.when(pl.program_id(0) == 0)
    def _():
        tiles = real_tiles(hx_ref)
        tiles[0] = jnp.concatenate([hm_ref[...], tiles[0]], axis=0)
        a_tiles = None
        if mixer_out:
            a_tiles = real_tiles(ax_ref)
            a_tiles[0] = jnp.concatenate([am_ref[...], a_tiles[0]], axis=0)
        outs = ffn(tiles, a_tiles)
        om_ref[...] = outs[0][:N_META]
        outs[0] = outs[0][N_META:]
        store_real(outs)

    @pl.when(pl.program_id(0) > 0)
    def _():
        store_real(ffn(real_tiles(hx_ref), real_tiles(ax_ref) if mixer_out else None))


def _slab_rows(n_rows, steps):
    for rows in range(BF16_ROWS, n_rows + 1, BF16_ROWS):
        if n_rows % rows == 0 and n_rows // rows <= steps:
            return rows
    raise ValueError((n_rows, steps))


def _ffn(hx, hm, g, wgu, wdn, layer, tm, casts=(), mixer_out=None):
    rows = hx.shape[0]
    steps = rows // tm
    tile = pl.BlockSpec((tm, D_MODEL), lambda i: (i, 0))
    in_specs = [tile, _resident((N_META, D_MODEL)), _layer_block((2, D_MODEL), layer),
                _resident((D_MODEL, 2 * D_FF)), _resident((D_FF, D_MODEL))]
    out_specs = [tile, pl.BlockSpec((N_META, D_MODEL), lambda i: (0, 0))]
    out_shape = [jax.ShapeDtypeStruct((rows, D_MODEL), F32), jax.ShapeDtypeStruct((N_META, D_MODEL), F32)]
    args = [hx, hm, g, wgu, wdn]
    if mixer_out is not None:
        in_specs += [tile, _resident((N_META, D_MODEL)), _layer_block((2, D_MODEL), layer),
                     _resident((D_MODEL, D_MODEL))]
        args += list(mixer_out)
    for w32, index in casts:
        _, n_rows, n_cols = w32.shape
        slab = _slab_rows(n_rows, steps)
        last = n_rows // slab - 1
        in_specs.append(pl.BlockSpec((None, slab, n_cols),
                                     lambda i, index=index, last=last: (index, jnp.minimum(i, last), 0)))
        out_specs.append(pl.BlockSpec((slab, n_cols), lambda i, last=last: (jnp.minimum(i, last), 0)))
        out_shape.append(jax.ShapeDtypeStruct((n_rows, n_cols), BF16))
    return pl.pallas_call(
        functools.partial(_ffn_kernel, n_casts=len(casts), mixer_out=mixer_out is not None),
        grid=(steps,),
        in_specs=in_specs,
        out_specs=out_specs,
        out_shape=out_shape,
        compiler_params=_params(1),
        name="ffn",
    )(*args, *[c[0] for c in casts])


MIXER_SUB_ROWS = 128


def _conv_kernel(h_ref, g_ref, tail_in_ref, win_ref, cw_ref, wout_ref,
                 o_ref, tail_out_ref, carry_ref, *, tm):
    @pl.when(pl.program_id(1) == 0)
    def _():
        carry_ref[...] = tail_in_ref[...]

    g = g_ref[...]
    cw = cw_ref[...]
    n_sub = max(1, tm // MIXER_SUB_ROWS)
    sub = tm // n_sub
    hs = [h_ref[0, s * sub:(s + 1) * sub] for s in range(n_sub)]
    hns = [_rms(h, g[0:1], NORM_EPS).astype(BF16) for h in hs]
    bcxs = [jnp.dot(hn, win_ref[...], preferred_element_type=F32) for hn in hns]
    row = lax.broadcasted_iota(jnp.int32, (sub, 1), 0)
    prev = carry_ref[...]
    ys = []
    for bcx in bcxs:
        u = bcx[:, D_MODEL:2 * D_MODEL] * bcx[:, 2 * D_MODEL:]
        last = prev[CARRY_ROWS - 1:CARRY_ROWS]
        u1 = jnp.where(row == 0, last, pltpu.roll(u, 1, 0))
        u2 = jnp.where(row == 0, prev[CARRY_ROWS - 2:CARRY_ROWS - 1],
                       jnp.where(row == 1, last, pltpu.roll(u, 2, 0)))
        conv = cw[0:1] * u2 + cw[1:2] * u1 + cw[2:3] * u
        ys.append((bcx[:, :D_MODEL] * conv).astype(BF16))
        prev = u[sub - CARRY_ROWS:]
    carry_ref[...] = prev
    tail_out_ref[0] = prev
    ms = [jnp.dot(y, wout_ref[...], preferred_element_type=F32) for y in ys]
    for s in range(n_sub):
        o_ref[0, s * sub:(s + 1) * sub] = hs[s] + _rms(ms[s], g[1:2], NORM_EPS)


def _conv_mixer(h, g, tail_in, win, cw, wout, layer, mixer, tm):
    bsz, seq, _ = h.shape
    return pl.pallas_call(
        functools.partial(_conv_kernel, tm=tm),
        grid=(bsz, seq // tm),
        in_specs=[
            pl.BlockSpec((1, tm, D_MODEL), lambda b, j: (b, j, 0)),
            _layer_block((2, D_MODEL), layer),
            _resident((CARRY_ROWS, D_MODEL)),
            _resident((D_MODEL, 3 * D_MODEL)),
            _layer_block((CONV_WIDTH, D_MODEL), mixer),
            _resident((D_MODEL, D_MODEL)),
        ],
        out_specs=[
            pl.BlockSpec((1, tm, D_MODEL), lambda b, j: (b, j, 0)),
            pl.BlockSpec((1, CARRY_ROWS, D_MODEL), lambda b, j: (b, 0, 0)),
        ],
        out_shape=[
            jax.ShapeDtypeStruct((bsz, seq, D_MODEL), F32),
            jax.ShapeDtypeStruct((bsz, CARRY_ROWS, D_MODEL), F32),
        ],
        scratch_shapes=[pltpu.VMEM((CARRY_ROWS, D_MODEL), F32)],
        compiler_params=_params(2),
        name="conv_mixer",
    )(h, g, tail_in, win, cw, wout)


NT_DIMS = (((1,), (1,)), ((), ()))

ONES_ROWS = 16
VT_ROWS = V_DIM + ONES_ROWS
Q_SCALE = HEAD_DIM ** -0.5 * math.log2(math.e)


def _qkv_kernel(h_ref, g_ref, w_ref, cos_ref, sin_lo_ref, sin_hi_ref, q2_ref, k_ref, vt_ref):
    g = g_ref[...]
    hn = _rms(h_ref[0], g[0:1], NORM_EPS).astype(BF16)
    qk = jnp.dot(hn, w_ref[...], preferred_element_type=F32)
    cos = cos_ref[...]
    sin_lo = sin_lo_ref[...]
    sin_hi = sin_hi_ref[...]
    half = ROT_DIM // 2
    first_component = lax.broadcasted_iota(jnp.int32, (1, LANES), 1) < HEAD_DIM
    for blk in range(2 * N_HEADS):
        x = qk[:, blk * LANES:(blk + 1) * LANES]
        r = x * cos + pltpu.roll(x, LANES - half, 1) * sin_lo + pltpu.roll(x, half, 1) * sin_hi
        if blk < N_HEADS:
            r = (r * Q_SCALE).astype(BF16)
            zero = jnp.zeros_like(r)
            q2_ref[0, 0, :, blk * LANES:(blk + 1) * LANES] = jnp.where(first_component, r, zero)
            q2_ref[0, 1, :, blk * LANES:(blk + 1) * LANES] = jnp.where(first_component, zero, r)
        else:
            k_ref[0, :, (blk - N_HEADS) * LANES:(blk - N_HEADS + 1) * LANES] = r.astype(BF16)
    ones = jnp.ones((ONES_ROWS, qk.shape[0]), BF16)
    for head in range(N_HEADS):
        v = qk[:, 2 * D_MODEL + head * V_DIM:2 * D_MODEL + (head + 1) * V_DIM]
        vt_ref[0, head, :V_DIM, :] = v.T.astype(BF16)
        vt_ref[0, head, V_DIM:, :] = ones


def _qkv_proj(h, g, w, cos, sin_lo, sin_hi, layer, tm):
    bsz, seq, _ = h.shape
    table = pl.BlockSpec((tm, LANES), lambda b, j: (j, 0))
    return pl.pallas_call(
        _qkv_kernel,
        grid=(bsz, seq // tm),
        in_specs=[
            pl.BlockSpec((1, tm, D_MODEL), lambda b, j: (b, j, 0)),
            _layer_block((2, D_MODEL), layer),
            _resident((D_MODEL, 3 * D_MODEL)),
            table, table, table,
        ],
        out_specs=[
            pl.BlockSpec((1, 2, tm, D_MODEL), lambda b, j: (b, 0, j, 0)),
            pl.BlockSpec((1, tm, D_MODEL), lambda b, j: (b, j, 0)),
            pl.BlockSpec((1, N_HEADS, VT_ROWS, tm), lambda b, j: (b, 0, 0, j)),
        ],
        out_shape=[
            jax.ShapeDtypeStruct((bsz, 2, seq, D_MODEL), BF16),
            jax.ShapeDtypeStruct((bsz, seq, D_MODEL), BF16),
            jax.ShapeDtypeStruct((bsz, N_HEADS, VT_ROWS, seq), BF16),
        ],
        compiler_params=_params(2),
        name="qkv_proj",
    )(h, g, w, cos, sin_lo, sin_hi)


SCORE_LOOKAHEAD = 2


def _attn_t_kernel(q2_ref, k_ref, vt_ref, kp_ref, vtp_ref, sg_ref, lam_ref, o_ref, acc_ref, st_ref, st0_ref,
                   *, tq, n_heads, lambda_init):
    seq = k_ref.shape[1]
    ahead = min(SCORE_LOOKAHEAD, n_heads)
    key_i = lax.broadcasted_iota(jnp.int32, (tq, 2 * tq), 0)
    qry_i = lax.broadcasted_iota(jnp.int32, (tq, 2 * tq), 1)
    causal = key_i <= jnp.where(qry_i >= tq, qry_i - tq, qry_i)
    lam_p = lam_ref[...]
    lam = (jnp.exp(jnp.sum(lam_p[0:1] * lam_p[1:2], axis=1, keepdims=True))
           - jnp.exp(jnp.sum(lam_p[2:3] * lam_p[3:4], axis=1, keepdims=True)) + lambda_init)
    sg = sg_ref[...]
    heads = [slice(g * LANES, (g + 1) * LANES) for g in range(n_heads)]

    def stacked_queries(start, g):
        return jnp.concatenate([q2_ref[0, 0, pl.ds(start, tq), heads[g]],
                                q2_ref[0, 1, pl.ds(start, tq), heads[g]]], axis=0)

    def first_scores(start, g, qq):
        k_all = jnp.concatenate([k_ref[0, pl.ds(start, tq), heads[g]], kp_ref[:, heads[g]]], axis=0)
        return lax.dot_general(k_all, qq, NT_DIMS, preferred_element_type=F32)

    def q_tile(qi, _):
        q_start = pl.multiple_of(qi * tq, tq)
        qqs = [stacked_queries(q_start, g) for g in range(n_heads)]

        def scores(g, start):
            return lax.dot_general(k_ref[0, pl.ds(start, tq), heads[g]], qqs[g], NT_DIMS,
                                   preferred_element_type=F32)

        def fold(start, next_start, maxes):
            first = maxes is None
            pending = [None] * ahead
            new_maxes = []
            for g in range(n_heads):
                st = pending.pop(0)
                if st is None:
                    st = st0_ref[g] if first else st_ref[g]
                upcoming = g + ahead
                if upcoming < n_heads:
                    pending.append(first_scores(q_start, upcoming, qqs[upcoming]) if first
                                   else scores(upcoming, start))
                else:
                    st_ref[upcoming - n_heads] = scores(upcoming - n_heads, next_start)
                vt = vt_ref[0, g, :, pl.ds(start, tq)]
                if first:
                    st_p, st = st[tq:], st[:tq]
                    st = jnp.where(causal, st, NEG_INF)
                    m_new = jnp.maximum(jnp.max(st, axis=0, keepdims=True), jnp.max(st_p, axis=0, keepdims=True))
                    p_all = jnp.concatenate([jnp.exp2(st - m_new).astype(BF16),
                                             jnp.exp2(st_p - m_new).astype(BF16)], axis=0)
                    vt_all = jnp.concatenate([vt, vtp_ref[g]], axis=1)
                    acc_ref[g] = jnp.dot(vt_all, p_all, preferred_element_type=F32)
                else:
                    m_new = jnp.maximum(maxes[g], jnp.max(st, axis=0, keepdims=True))
                    alpha = jnp.exp2(maxes[g] - m_new)
                    pv = jnp.dot(vt, jnp.exp2(st - m_new).astype(BF16), preferred_element_type=F32)
                    acc_ref[g] = alpha * acc_ref[g] + pv
                new_maxes.append(m_new)
            return tuple(new_maxes)

        def block(kb, maxes):
            start = pl.multiple_of(kb * tq, tq)
            return fold(start, start + tq, maxes)

        lax.fori_loop(0, qi, block, fold(q_start, 0, None))

        next_q = pl.multiple_of(jnp.minimum(q_start + tq, seq - tq), tq)
        for g in range(ahead):
            st0_ref[g] = first_scores(next_q, g, stacked_queries(next_q, g))

        for g, hs in enumerate(heads):
            acc = acc_ref[g]
            pv = acc[:V_DIM] * (1.0 / acc[V_DIM:V_DIM + 1])
            o = pv[:, :tq] - lam * pv[:, tq:]
            o = o * lax.rsqrt(jnp.mean(o * o, axis=0, keepdims=True) + SUBLN_EPS) * sg * (1.0 - lambda_init)
            o_ref[0, pl.ds(q_start, tq), hs] = o.T.astype(BF16)
        return 0

    for g in range(ahead):
        st0_ref[g] = first_scores(0, g, stacked_queries(0, g))
    lax.fori_loop(0, seq // tq, q_tile, 0)


def _attention_t(q2, k, vt, lam_p, subln_g, prefix_k, prefix_vt, mixer, tq, n_heads, lambda_init):
    bsz, seq, _ = k.shape
    n_prefix = prefix_k.shape[0]
    groups = N_HEADS // n_heads
    width = n_heads * LANES
    return pl.pallas_call(
        functools.partial(_attn_t_kernel, tq=tq, n_heads=n_heads, lambda_init=lambda_init),
        grid=(bsz, groups),
        in_specs=[
            pl.BlockSpec((1, 2, seq, width), lambda b, h: (b, 0, 0, h)),
            pl.BlockSpec((1, seq, width), lambda b, h: (b, 0, h)),
            pl.BlockSpec((1, n_heads, VT_ROWS, seq), lambda b, h: (b, h, 0, 0)),
            pl.BlockSpec((n_prefix, width), lambda b, h: (0, h)),
            pl.BlockSpec((n_heads, VT_ROWS, n_prefix), lambda b, h: (h, 0, 0)),
            _layer_block((V_DIM, 1), mixer),
            _layer_block((4, HEAD_DIM), mixer),
        ],
        out_specs=pl.BlockSpec((1, seq, width), lambda b, h: (b, 0, h)),
        out_shape=jax.ShapeDtypeStruct((bsz, seq, D_MODEL), BF16),
        scratch_shapes=[pltpu.VMEM((n_heads, VT_ROWS, 2 * tq), F32),
                        pltpu.VMEM((min(SCORE_LOOKAHEAD, n_heads), tq, 2 * tq), F32),
                        pltpu.VMEM((min(SCORE_LOOKAHEAD, n_heads), tq + n_prefix, 2 * tq), F32)],
        compiler_params=_params(2),
        name="diff_attention",
    )(q2, k, vt, prefix_k, prefix_vt, subln_g.reshape(-1, V_DIM, 1), lam_p)


def _attn_small_kernel(lam_ref, sg_ref, q2_ref, k_ref, v_ref, o_ref, *, lambda_init):
    tq = k_ref.shape[0]
    qq = jnp.concatenate([q2_ref[0], q2_ref[1]], axis=0)
    s = lax.dot_general(qq, k_ref[...], NT_DIMS, preferred_element_type=F32)
    r = lax.broadcasted_iota(jnp.int32, (2 * tq, tq), 0)
    c = lax.broadcasted_iota(jnp.int32, (2 * tq, tq), 1)
    s = jnp.where(c <= jnp.where(r >= tq, r - tq, r), s, NEG_INF)
    p = jnp.exp2(s - jnp.max(s, axis=1, keepdims=True))
    l = jnp.sum(p, axis=1, keepdims=True)
    pv = jnp.dot(p.astype(BF16), v_ref[...], preferred_element_type=F32) / l
    lam_p = lam_ref[...]
    lam = (jnp.exp(jnp.sum(lam_p[0:1] * lam_p[1:2], axis=1, keepdims=True))
           - jnp.exp(jnp.sum(lam_p[2:3] * lam_p[3:4], axis=1, keepdims=True)) + lambda_init)
    o = pv[:tq] - lam * pv[tq:]
    o_ref[...] = (_rms(o, sg_ref[...], SUBLN_EPS) * (1.0 - lambda_init)).astype(BF16)


def _attention_small(q2, k, v, lam_p, subln_g, mixer, lambda_init):
    seq = k.shape[0]
    return pl.pallas_call(
        functools.partial(_attn_small_kernel, lambda_init=lambda_init),
        grid=(N_HEADS,),
        in_specs=[
            _layer_block((4, HEAD_DIM), mixer),
            _layer_block((1, V_DIM), mixer),
            pl.BlockSpec((2, seq, LANES), lambda h: (0, 0, h)),
            pl.BlockSpec((seq, LANES), lambda h: (0, h)),
            pl.BlockSpec((seq, LANES), lambda h: (0, h)),
        ],
        out_specs=pl.BlockSpec((seq, LANES), lambda h: (0, h)),
        out_shape=jax.ShapeDtypeStruct((seq, D_MODEL), BF16),
        compiler_params=_params(1),
        name="diff_attention_meta",
    )(lam_p, subln_g.reshape(-1, 1, V_DIM), q2, k, v)


def _rope_tables(first_pos, n_pos):
    half = ROT_DIM // 2
    pos = jnp.arange(first_pos, first_pos + n_pos, dtype=F32)
    inv_freq = ROPE_THETA ** (-jnp.arange(0, ROT_DIM, 2, dtype=F32) / ROT_DIM)
    ang = pos[:, None] * inv_freq[None, :]
    cos, sin = jnp.cos(ang), jnp.sin(ang)
    ones = jnp.ones((n_pos, HEAD_DIM - ROT_DIM), F32)
    zeros = jnp.zeros((n_pos, HEAD_DIM - half), F32)
    cos64 = jnp.concatenate([cos, cos, ones], axis=1)
    lo64 = jnp.concatenate([-sin, zeros], axis=1)
    hi64 = jnp.concatenate([jnp.zeros((n_pos, half), F32), sin, ones * 0.0], axis=1)
    return tuple(jnp.concatenate([t, t], axis=1) for t in (cos64, lo64, hi64))


def _attn_mixer(hx, hm, g, wqkv, lam_p, subln_g, layer, mixer, tm, tq, lambda_init):
    bsz, seq, _ = hx.shape
    q2_m, k_m, vt_m = (t[0] for t in _qkv_proj(hm, g, wqkv, *_rope_tables(0, N_META), layer, N_META))
    v_m = jnp.transpose(vt_m[:, :V_DIM, :], (2, 0, 1)).reshape(N_META, D_MODEL)
    a_m = _attention_small(q2_m, k_m, v_m, lam_p, subln_g, mixer, lambda_init)
    q2, k, vt = _qkv_proj(hx, g, wqkv, *_rope_tables(N_META, seq), layer, tm)
    a = _attention_t(q2, k, vt, lam_p, subln_g, k_m, vt_m, mixer, tq, HEADS_PER_STEP, lambda_init)
    return a, a_m


ROW_TILE = 512
FFN_ROW_TILE = 512
Q_TILE = 256
HEADS_PER_STEP = 8


def kernel(x, meta_tokens, ln_ffn1, ffn1_w_gu, ffn1_w_down, ln_mix, conv_w_in, conv_w, conv_w_out,
           attn_w_qkv, attn_lambda, attn_subln_g, attn_w_o, ln_ffn2, ffn2_w_gu, ffn2_w_down):
    bsz, seq, d = x.shape
    assert d == D_MODEL and seq % ROW_TILE == 0 and seq % Q_TILE == 0 and (bsz * seq) % FFN_ROW_TILE == 0
    hx = x
    hm = meta_tokens.astype(x.dtype)[None]

    ffns = [(ln, wgu, wdn, i) for i in range(DEPTH)
            for ln, wgu, wdn in ((ln_ffn1, ffn1_w_gu, ffn1_w_down), (ln_ffn2, ffn2_w_gu, ffn2_w_down))]
    ffn_w16 = [ffns[0][1][0].astype(BF16), ffns[0][2][0].astype(BF16)]
    mixer_w32 = [[(conv_w_in, i // N_MIXERS), (conv_w_out, i // N_MIXERS)] if i % N_MIXERS == 0 else
                 [(attn_w_qkv, i // N_MIXERS), (attn_w_o, i // N_MIXERS)] for i in range(DEPTH)]
    mixer_w16 = {}

    def run_ffn(n, hx, hm, mixer_out=None):
        nonlocal ffn_w16
        ln, _, _, layer = ffns[n]
        mixer_layer = 0 if n == 0 else (n + 1) // 2 if n % 2 == 1 and (n + 1) // 2 < DEPTH else None
        casts = list(mixer_w32[mixer_layer]) if mixer_layer is not None else []
        n_mixer = len(casts)
        if n + 1 < len(ffns):
            _, wgu32, wdn32, nxt = ffns[n + 1]
            casts += [(wgu32, nxt), (wdn32, nxt)]
        hx, hm, *w16 = _ffn(hx.reshape(bsz * seq, D_MODEL), hm[0], ln, *ffn_w16, layer, FFN_ROW_TILE, casts,
                            mixer_out)
        if mixer_layer is not None:
            mixer_w16[mixer_layer] = w16[:n_mixer]
        ffn_w16 = w16[n_mixer:]
        return hx.reshape(bsz, seq, D_MODEL), hm[None]

    no_history = jnp.zeros((CARRY_ROWS, D_MODEL), F32)
    for i in range(DEPTH):
        j = i // N_MIXERS
        hx, hm = run_ffn(2 * i, hx, hm)
        if i % N_MIXERS == 0:
            win, wout = mixer_w16[i]
            hm, meta_tail = _conv_mixer(hm, ln_mix, no_history, win, conv_w, wout, i, j, N_META)
            hx, _ = _conv_mixer(hx, ln_mix, meta_tail[0], win, conv_w, wout, i, j, ROW_TILE)
            mixer_out = None
        else:
            wqkv, wo = mixer_w16[i]
            lambda_init = 0.8 - 0.6 * math.exp(-0.3 * i)
            a, a_m = _attn_mixer(hx, hm, ln_mix, wqkv, attn_lambda, attn_subln_g,
                                 i, j, ROW_TILE, Q_TILE, lambda_init)
            mixer_out = (a.reshape(bsz * seq, D_MODEL), a_m, ln_mix, wo)
        hx, hm = run_ffn(2 * i + 1, hx, hm, mixer_out=mixer_out)
    return hx
```

```python
import functools
import math

import jax
import jax.numpy as jnp
from jax import lax
from jax.experimental import pallas as pl
from jax.experimental.pallas import tpu as pltpu

D_MODEL = 1024
DEPTH = 4
N_MIXERS = 2
N_META = 16
D_FF = 2816
CONV_WIDTH = 3
HEAD_DIM = 64
V_DIM = 2 * HEAD_DIM
N_HEADS = D_MODEL // V_DIM
ROT_DIM = HEAD_DIM // 4
ROPE_THETA = 500000.0
NORM_EPS = 1e-6
SUBLN_EPS = 1e-5
NEG_INF = -1e30

F32 = jnp.float32
BF16 = jnp.bfloat16

V7X_VMEM_LIMIT_BYTES = 60 * 1024 * 1024
SUBLANES = 8
LANES = 128
CARRY_ROWS = SUBLANES


def _rms(t, g, eps):
    return t * lax.rsqrt(jnp.mean(t * t, axis=-1, keepdims=True) + eps) * g


def _resident(shape):
    zeros = (0,) * len(shape)
    return pl.BlockSpec(shape, lambda *_: zeros, pipeline_mode=pl.Buffered(1))


def _layer_block(shape, layer):
    zeros = (0,) * len(shape)
    return pl.BlockSpec((None,) + tuple(shape), lambda *_: (layer,) + zeros, pipeline_mode=pl.Buffered(1))


def _params(n_grid_dims):
    return pltpu.CompilerParams(
        dimension_semantics=("arbitrary",) * n_grid_dims,
        vmem_limit_bytes=V7X_VMEM_LIMIT_BYTES)


FFN_SUB_ROWS = 128
BF16_ROWS = 16


def _ffn_kernel(*refs, n_casts, mixer_out):
    hx_ref, hm_ref, g_ref, wgu_ref, wdn_ref = refs[:5]
    n_in = 5
    if mixer_out:
        ax_ref, am_ref, gmix_ref, wo_ref = refs[n_in:n_in + 4]
        n_in += 4
    w32_refs = refs[n_in:n_in + n_casts]
    ox_ref, om_ref = refs[n_in + n_casts:n_in + n_casts + 2]
    w16_refs = refs[n_in + n_casts + 2:]
    for w32_ref, w16_ref in zip(w32_refs, w16_refs):
        w16_ref[...] = w32_ref[...].astype(BF16)
    g = g_ref[...]
    sub = FFN_SUB_ROWS
    n_sub = hx_ref.shape[0] // sub

    def ffn(tiles, a_tiles):
        if mixer_out:
            ms = [jnp.dot(a, wo_ref[...], preferred_element_type=F32) for a in a_tiles]
            tiles = [h + _rms(m, gmix_ref[...][1:2], NORM_EPS) for h, m in zip(tiles, ms)]
        hns = [_rms(h, g[0:1], NORM_EPS).astype(BF16) for h in tiles]
        gus = [jnp.dot(hn, wgu_ref[...], preferred_element_type=F32) for hn in hns]
        acts = [(gu[:, :D_FF] * jax.nn.sigmoid(gu[:, :D_FF]) * gu[:, D_FF:]).astype(BF16) for gu in gus]
        fs = [jnp.dot(act, wdn_ref[...], preferred_element_type=F32) for act in acts]
        return [h + 0.5 * _rms(f, g[1:2], NORM_EPS) for h, f in zip(tiles, fs)]

    def real_tiles(x_ref):
        return [x_ref[s * sub:(s + 1) * sub] for s in range(n_sub)]

    def store_real(outs):
        for s, o in enumerate(outs):
            ox_ref[s * sub:(s + 1) * sub] = o

    @pl.when(pl.program_id(0) == 0)
    def _():
        tiles = real_tiles(hx_ref)
        tiles[0] = jnp.concatenate([hm_ref[...], tiles[0]], axis=0)
        a_tiles = None
        if mixer_out:
            a_tiles = real_tiles(ax_ref)
            a_tiles[0] = jnp.concatenate([am_ref[...], a_tiles[0]], axis=0)
        outs = ffn(tiles, a_tiles)
        om_ref[...] = outs[0][:N_META]
        outs[0] = outs[0][N_META:]
        store_real(outs)

    @pl.when(pl.program_id(0) > 0)
    def _():
        store_real(ffn(real_tiles(hx_ref), real_tiles(ax_ref) if mixer_out else None))


def _slab_rows(n_rows, steps):
    for rows in range(BF16_ROWS, n_rows + 1, BF16_ROWS):
        if n_rows % rows == 0 and n_rows // rows <= steps:
            return rows
    raise ValueError((n_rows, steps))


def _ffn(hx, hm, g, wgu, wdn, layer, tm, casts=(), mixer_out=None):
    rows = hx.shape[0]
    steps = rows // tm
    tile = pl.BlockSpec((tm, D_MODEL), lambda i: (i, 0))
    in_specs = [tile, _resident((N_META, D_MODEL)), _layer_block((2, D_MODEL), layer),
                _resident((D_MODEL, 2 * D_FF)), _resident((D_FF, D_MODEL))]
    out_specs = [tile, pl.BlockSpec((N_META, D_MODEL), lambda i: (0, 0))]
    out_shape = [jax.ShapeDtypeStruct((rows, D_MODEL), F32), jax.ShapeDtypeStruct((N_META, D_MODEL), F32)]
    args = [hx, hm, g, wgu, wdn]
    if mixer_out is not None:
        in_specs += [tile, _resident((N_META, D_MODEL)), _layer_block((2, D_MODEL), layer),
                     _resident((D_MODEL, D_MODEL))]
        args += list(mixer_out)
    for w32, index in casts:
        _, n_rows, n_cols = w32.shape
        slab = _slab_rows(n_rows, steps)
        last = n_rows // slab - 1
        in_specs.append(pl.BlockSpec((None, slab, n_cols),
                                     lambda i, index=index, last=last: (index, jnp.minimum(i, last), 0)))
        out_specs.append(pl.BlockSpec((slab, n_cols), lambda i, last=last: (jnp.minimum(i, last), 0)))
        out_shape.append(jax.ShapeDtypeStruct((n_rows, n_cols), BF16))
    return pl.pallas_call(
        functools.partial(_ffn_kernel, n_casts=len(casts), mixer_out=mixer_out is not None),
        grid=(steps,),
        in_specs=in_specs,
        out_specs=out_specs,
        out_shape=out_shape,
        compiler_params=_params(1),
        name="ffn",
    )(*args, *[c[0] for c in casts])


MIXER_SUB_ROWS = 128


def _conv_kernel(h_ref, g_ref, tail_in_ref, win_ref, cw_ref, wout_ref,
                 o_ref, tail_out_ref, carry_ref, *, tm):
    @pl.when(pl.program_id(1) == 0)
    def _():
        carry_ref[...] = tail_in_ref[...]

    g = g_ref[...]
    cw = cw_ref[...]
    n_sub = max(1, tm // MIXER_SUB_ROWS)
    sub = tm // n_sub
    hs = [h_ref[0, s * sub:(s + 1) * sub] for s in range(n_sub)]
    hns = [_rms(h, g[0:1], NORM_EPS).astype(BF16) for h in hs]
    bcxs = [jnp.dot(hn, win_ref[...], preferred_element_type=F32) for hn in hns]
    row = lax.broadcasted_iota(jnp.int32, (sub, 1), 0)
    prev = carry_ref[...]
    ys = []
    for bcx in bcxs:
        u = bcx[:, D_MODEL:2 * D_MODEL] * bcx[:, 2 * D_MODEL:]
        last = prev[CARRY_ROWS - 1:CARRY_ROWS]
        u1 = jnp.where(row == 0, last, pltpu.roll(u, 1, 0))
        u2 = jnp.where(row == 0, prev[CARRY_ROWS - 2:CARRY_ROWS - 1],
                       jnp.where(row == 1, last, pltpu.roll(u, 2, 0)))
        conv = cw[0:1] * u2 + cw[1:2] * u1 + cw[2:3] * u
        ys.append((bcx[:, :D_MODEL] * conv).astype(BF16))
        prev = u[sub - CARRY_ROWS:]
    carry_ref[...] = prev
    tail_out_ref[0] = prev
    ms = [jnp.dot(y, wout_ref[...], preferred_element_type=F32) for y in ys]
    for s in range(n_sub):
        o_ref[0, s * sub:(s + 1) * sub] = hs[s] + _rms(ms[s], g[1:2], NORM_EPS)


def _conv_mixer(h, g, tail_in, win, cw, wout, layer, mixer, tm):
    bsz, seq, _ = h.shape
    return pl.pallas_call(
        functools.partial(_conv_kernel, tm=tm),
        grid=(bsz, seq // tm),
        in_specs=[
            pl.BlockSpec((1, tm, D_MODEL), lambda b, j: (b, j, 0)),
            _layer_block((2, D_MODEL), layer),
            _resident((CARRY_ROWS, D_MODEL)),
            _resident((D_MODEL, 3 * D_MODEL)),
            _layer_block((CONV_WIDTH, D_MODEL), mixer),
            _resident((D_MODEL, D_MODEL)),
        ],
        out_specs=[
            pl.BlockSpec((1, tm, D_MODEL), lambda b, j: (b, j, 0)),
            pl.BlockSpec((1, CARRY_ROWS, D_MODEL), lambda b, j: (b, 0, 0)),
        ],
        out_shape=[
            jax.ShapeDtypeStruct((bsz, seq, D_MODEL), F32),
            jax.ShapeDtypeStruct((bsz, CARRY_ROWS, D_MODEL), F32),
        ],
        scratch_shapes=[pltpu.VMEM((CARRY_ROWS, D_MODEL), F32)],
        compiler_params=_params(2),
        name="conv_mixer",
    )(h, g, tail_in, win, cw, wout)


NT_DIMS = (((1,), (1,)), ((), ()))

ONES_ROWS = 16
VT_ROWS = V_DIM + ONES_ROWS
Q_SCALE = HEAD_DIM ** -0.5 * math.log2(math.e)


def _qkv_kernel(h_ref, g_ref, w_ref, cos_ref, sin_lo_ref, sin_hi_ref, q2_ref, k_ref, vt_ref):
    g = g_ref[...]
    hn = _rms(h_ref[0], g[0:1], NORM_EPS).astype(BF16)
    qk = jnp.dot(hn, w_ref[...], preferred_element_type=F32)
    cos = cos_ref[...]
    sin_lo = sin_lo_ref[...]
    sin_hi = sin_hi_ref[...]
    half = ROT_DIM // 2
    first_component = lax.broadcasted_iota(jnp.int32, (1, LANES), 1) < HEAD_DIM
    for blk in range(2 * N_HEADS):
        x = qk[:, blk * LANES:(blk + 1) * LANES]
        r = x * cos + pltpu.roll(x, LANES - half, 1) * sin_lo + pltpu.roll(x, half, 1) * sin_hi
        if blk < N_HEADS:
            r = (r * Q_SCALE).astype(BF16)
            zero = jnp.zeros_like(r)
            q2_ref[0, 0, :, blk * LANES:(blk + 1) * LANES] = jnp.where(first_component, r, zero)
            q2_ref[0, 1, :, blk * LANES:(blk + 1) * LANES] = jnp.where(first_component, zero, r)
        else:
            k_ref[0, :, (blk - N_HEADS) * LANES:(blk - N_HEADS + 1) * LANES] = r.astype(BF16)
    ones = jnp.ones((ONES_ROWS, qk.shape[0]), BF16)
    for head in range(N_HEADS):
        v = qk[:, 2 * D_MODEL + head * V_DIM:2 * D_MODEL + (head + 1) * V_DIM]
        vt_ref[0, head, :V_DIM, :] = v.T.astype(BF16)
        vt_ref[0, head, V_DIM:, :] = ones


def _qkv_proj(h, g, w, cos, sin_lo, sin_hi, layer, tm):
    bsz, seq, _ = h.shape
    table = pl.BlockSpec((tm, LANES), lambda b, j: (j, 0))
    return pl.pallas_call(
        _qkv_kernel,
        grid=(bsz, seq // tm),
        in_specs=[
            pl.BlockSpec((1, tm, D_MODEL), lambda b, j: (b, j, 0)),
            _layer_block((2, D_MODEL), layer),
            _resident((D_MODEL, 3 * D_MODEL)),
            table, table, table,
        ],
        out_specs=[
            pl.BlockSpec((1, 2, tm, D_MODEL), lambda b, j: (b, 0, j, 0)),
            pl.BlockSpec((1, tm, D_MODEL), lambda b, j: (b, j, 0)),
            pl.BlockSpec((1, N_HEADS, VT_ROWS, tm), lambda b, j: (b, 0, 0, j)),
        ],
        out_shape=[
            jax.ShapeDtypeStruct((bsz, 2, seq, D_MODEL), BF16),
            jax.ShapeDtypeStruct((bsz, seq, D_MODEL), BF16),
            jax.ShapeDtypeStruct((bsz, N_HEADS, VT_ROWS, seq), BF16),
        ],
        compiler_params=_params(2),
        name="qkv_proj",
    )(h, g, w, cos, sin_lo, sin_hi)


SCORE_LOOKAHEAD = 3


def _attn_t_kernel(q2_ref, k_ref, vt_ref, kp_ref, vtp_ref, sg_ref, lam_ref, o_ref, acc_ref, st_ref,
                   *, tq, n_heads, lambda_init):
    seq = k_ref.shape[1]
    ahead = min(SCORE_LOOKAHEAD, n_heads)
    key_i = lax.broadcasted_iota(jnp.int32, (tq, 2 * tq), 0)
    qry_i = lax.broadcasted_iota(jnp.int32, (tq, 2 * tq), 1)
    causal = key_i <= jnp.where(qry_i >= tq, qry_i - tq, qry_i)
    lam_p = lam_ref[...]
    lam = (jnp.exp(jnp.sum(lam_p[0:1] * lam_p[1:2], axis=1, keepdims=True))
           - jnp.exp(jnp.sum(lam_p[2:3] * lam_p[3:4], axis=1, keepdims=True)) + lambda_init)
    sg = sg_ref[...]
    heads = [slice(g * LANES, (g + 1) * LANES) for g in range(n_heads)]

    def q_tile(qi, _):
        q_start = pl.multiple_of(qi * tq, tq)
        qqs = [jnp.concatenate([q2_ref[0, 0, pl.ds(q_start, tq), hs],
                                q2_ref[0, 1, pl.ds(q_start, tq), hs]], axis=0) for hs in heads]

        def scores(g, start):
            return lax.dot_general(k_ref[0, pl.ds(start, tq), heads[g]], qqs[g], NT_DIMS,
                                   preferred_element_type=F32)

        def first_scores(g):
            k_all = jnp.concatenate([k_ref[0, pl.ds(q_start, tq), heads[g]], kp_ref[:, heads[g]]], axis=0)
            st_all = lax.dot_general(k_all, qqs[g], NT_DIMS, preferred_element_type=F32)
            return st_all[tq:], st_all[:tq]

        def fold(start, next_start, maxes):
            first = maxes is None
            pending = [first_scores(g) for g in range(ahead)] if first else [None] * ahead
            new_maxes = []
            for g in range(n_heads):
                st = pending.pop(0)
                if st is None:
                    st = st_ref[g]
                upcoming = g + ahead
                if upcoming < n_heads:
                    pending.append(first_scores(upcoming) if first else scores(upcoming, start))
                else:
                    st_ref[upcoming - n_heads] = scores(upcoming - n_heads, next_start)
                vt = vt_ref[0, g, :, pl.ds(start, tq)]
                if first:
                    st_p, st = st
                    st = jnp.where(causal, st, NEG_INF)
                    m_new = jnp.maximum(jnp.max(st, axis=0, keepdims=True), jnp.max(st_p, axis=0, keepdims=True))
                    p_all = jnp.concatenate([jnp.exp2(st - m_new).astype(BF16),
                                             jnp.exp2(st_p - m_new).astype(BF16)], axis=0)
                    vt_all = jnp.concatenate([vt, vtp_ref[g]], axis=1)
                    acc_ref[g] = jnp.dot(vt_all, p_all, preferred_element_type=F32)
                else:
                    m_new = jnp.maximum(maxes[g], jnp.max(st, axis=0, keepdims=True))
                    alpha = jnp.exp2(maxes[g] - m_new)
                    pv = jnp.dot(vt, jnp.exp2(st - m_new).astype(BF16), preferred_element_type=F32)
                    acc_ref[g] = alpha * acc_ref[g] + pv
                new_maxes.append(m_new)
            return tuple(new_maxes)

        def block(kb, maxes):
            start = pl.multiple_of(kb * tq, tq)
            return fold(start, start + tq, maxes)

        lax.fori_loop(0, qi, block, fold(q_start, 0, None))

        for g, hs in enumerate(heads):
            acc = acc_ref[g]
            pv = acc[:V_DIM] * (1.0 / acc[V_DIM:V_DIM + 1])
            o = pv[:, :tq] - lam * pv[:, tq:]
            o = o * lax.rsqrt(jnp.mean(o * o, axis=0, keepdims=True) + SUBLN_EPS) * sg * (1.0 - lambda_init)
            o_ref[0, pl.ds(q_start, tq), hs] = o.T.astype(BF16)
        return 0

    lax.fori_loop(0, seq // tq, q_tile, 0)


def _attention_t(q2, k, vt, lam_p, subln_g, prefix_k, prefix_vt, mixer, tq, n_heads, lambda_init):
    bsz, seq, _ = k.shape
    n_prefix = prefix_k.shape[0]
    groups = N_HEADS // n_heads
    width = n_heads * LANES
    return pl.pallas_call(
        functools.partial(_attn_t_kernel, tq=tq, n_heads=n_heads, lambda_init=lambda_init),
        grid=(bsz, groups),
        in_specs=[
            pl.BlockSpec((1, 2, seq, width), lambda b, h: (b, 0, 0, h)),
            pl.BlockSpec((1, seq, width), lambda b, h: (b, 0, h)),
            pl.BlockSpec((1, n_heads, VT_ROWS, seq), lambda b, h: (b, h, 0, 0)),
            pl.BlockSpec((n_prefix, width), lambda b, h: (0, h)),
            pl.BlockSpec((n_heads, VT_ROWS, n_prefix), lambda b, h: (h, 0, 0)),
            _layer_block((V_DIM, 1), mixer),
            _layer_block((4, HEAD_DIM), mixer),
        ],
        out_specs=pl.BlockSpec((1, seq, width), lambda b, h: (b, 0, h)),
        out_shape=jax.ShapeDtypeStruct((bsz, seq, D_MODEL), BF16),
        scratch_shapes=[pltpu.VMEM((n_heads, VT_ROWS, 2 * tq), F32),
                        pltpu.VMEM((min(SCORE_LOOKAHEAD, n_heads), tq, 2 * tq), F32)],
        compiler_params=_params(2),
        name="diff_attention",
    )(q2, k, vt, prefix_k, prefix_vt, subln_g.reshape(-1, V_DIM, 1), lam_p)


def _attn_small_kernel(lam_ref, sg_ref, q2_ref, k_ref, v_ref, o_ref, *, lambda_init):
    tq = k_ref.shape[0]
    qq = jnp.concatenate([q2_ref[0], q2_ref[1]], axis=0)
    s = lax.dot_general(qq, k_ref[...], NT_DIMS, preferred_element_type=F32)
    r = lax.broadcasted_iota(jnp.int32, (2 * tq, tq), 0)
    c = lax.broadcasted_iota(jnp.int32, (2 * tq, tq), 1)
    s = jnp.where(c <= jnp.where(r >= tq, r - tq, r), s, NEG_INF)
    p = jnp.exp2(s - jnp.max(s, axis=1, keepdims=True))
    l = jnp.sum(p, axis=1, keepdims=True)
    pv = jnp.dot(p.astype(BF16), v_ref[...], preferred_element_type=F32) / l
    lam_p = lam_ref[...]
    lam = (jnp.exp(jnp.sum(lam_p[0:1] * lam_p[1:2], axis=1, keepdims=True))
           - jnp.exp(jnp.sum(lam_p[2:3] * lam_p[3:4], axis=1, keepdims=True)) + lambda_init)
    o = pv[:tq] - lam * pv[tq:]
    o_ref[...] = (_rms(o, sg_ref[...], SUBLN_EPS) * (1.0 - lambda_init)).astype(BF16)


def _attention_small(q2, k, v, lam_p, subln_g, mixer, lambda_init):
    seq = k.shape[0]
    return pl.pallas_call(
        functools.partial(_attn_small_kernel, lambda_init=lambda_init),
        grid=(N_HEADS,),
        in_specs=[
            _layer_block((4, HEAD_DIM), mixer),
            _layer_block((1, V_DIM), mixer),
            pl.BlockSpec((2, seq, LANES), lambda h: (0, 0, h)),
            pl.BlockSpec((seq, LANES), lambda h: (0, h)),
            pl.BlockSpec((seq, LANES), lambda h: (0, h)),
        ],
        out_specs=pl.BlockSpec((seq, LANES), lambda h: (0, h)),
        out_shape=jax.ShapeDtypeStruct((seq, D_MODEL), BF16),
        compiler_params=_params(1),
        name="diff_attention_meta",
    )(lam_p, subln_g.reshape(-1, 1, V_DIM), q2, k, v)


def _rope_tables(first_pos, n_pos):
    half = ROT_DIM // 2
    pos = jnp.arange(first_pos, first_pos + n_pos, dtype=F32)
    inv_freq = ROPE_THETA ** (-jnp.arange(0, ROT_DIM, 2, dtype=F32) / ROT_DIM)
    ang = pos[:, None] * inv_freq[None, :]
    cos, sin = jnp.cos(ang), jnp.sin(ang)
    ones = jnp.ones((n_pos, HEAD_DIM - ROT_DIM), F32)
    zeros = jnp.zeros((n_pos, HEAD_DIM - half), F32)
    cos64 = jnp.concatenate([cos, cos, ones], axis=1)
    lo64 = jnp.concatenate([-sin, zeros], axis=1)
    hi64 = jnp.concatenate([jnp.zeros((n_pos, half), F32), sin, ones * 0.0], axis=1)
    return tuple(jnp.concatenate([t, t], axis=1) for t in (cos64, lo64, hi64))


def _attn_mixer(hx, hm, g, wqkv, lam_p, subln_g, layer, mixer, tm, tq, lambda_init):
    bsz, seq, _ = hx.shape
    q2_m, k_m, vt_m = (t[0] for t in _qkv_proj(hm, g, wqkv, *_rope_tables(0, N_META), layer, N_META))
    v_m = jnp.transpose(vt_m[:, :V_DIM, :], (2, 0, 1)).reshape(N_META, D_MODEL)
    a_m = _attention_small(q2_m, k_m, v_m, lam_p, subln_g, mixer, lambda_init)
    q2, k, vt = _qkv_proj(hx, g, wqkv, *_rope_tables(N_META, seq), layer, tm)
    a = _attention_t(q2, k, vt, lam_p, subln_g, k_m, vt_m, mixer, tq, HEADS_PER_STEP, lambda_init)
    return a, a_m


ROW_TILE = 512
FFN_ROW_TILE = 512
Q_TILE = 256
HEADS_PER_STEP = 8


def kernel(x, meta_tokens, ln_ffn1, ffn1_w_gu, ffn1_w_down, ln_mix, conv_w_in, conv_w, conv_w_out,
           attn_w_qkv, attn_lambda, attn_subln_g, attn_w_o, ln_ffn2, ffn2_w_gu, ffn2_w_down):
    bsz, seq, d = x.shape
    assert d == D_MODEL and seq % ROW_TILE == 0 and seq % Q_TILE == 0 and (bsz * seq) % FFN_ROW_TILE == 0
    hx = x
    hm = meta_tokens.astype(x.dtype)[None]

    ffns = [(ln, wgu, wdn, i) for i in range(DEPTH)
            for ln, wgu, wdn in ((ln_ffn1, ffn1_w_gu, ffn1_w_down), (ln_ffn2, ffn2_w_gu, ffn2_w_down))]
    ffn_w16 = [ffns[0][1][0].astype(BF16), ffns[0][2][0].astype(BF16)]
    mixer_w32 = [[(conv_w_in, i // N_MIXERS), (conv_w_out, i // N_MIXERS)] if i % N_MIXERS == 0 else
                 [(attn_w_qkv, i // N_MIXERS), (attn_w_o, i // N_MIXERS)] for i in range(DEPTH)]
    mixer_w16 = {}

    def run_ffn(n, hx, hm, mixer_out=None):
        nonlocal ffn_w16
        ln, _, _, layer = ffns[n]
        mixer_layer = 0 if n == 0 else (n + 1) // 2 if n % 2 == 1 and (n + 1) // 2 < DEPTH else None
        casts = list(mixer_w32[mixer_layer]) if mixer_layer is not None else []
        n_mixer = len(casts)
        if n + 1 < len(ffns):
            _, wgu32, wdn32, nxt = ffns[n + 1]
            casts += [(wgu32, nxt), (wdn32, nxt)]
        hx, hm, *w16 = _ffn(hx.reshape(bsz * seq, D_MODEL), hm[0], ln, *ffn_w16, layer, FFN_ROW_TILE, casts,
                            mixer_out)
        if mixer_layer is not None:
            mixer_w16[mixer_layer] = w16[:n_mixer]
        ffn_w16 = w16[n_mixer:]
        return hx.reshape(bsz, seq, D_MODEL), hm[None]

    no_history = jnp.zeros((CARRY_ROWS, D_MODEL), F32)
    for i in range(DEPTH):
        j = i // N_MIXERS
        hx, hm = run_ffn(2 * i, hx, hm)
        if i % N_MIXERS == 0:
            win, wout = mixer_w16[i]
            hm, meta_tail = _conv_mixer(hm, ln_mix, no_history, win, conv_w, wout, i, j, N_META)
            hx, _ = _conv_mixer(hx, ln_mix, meta_tail[0], win, conv_w, wout, i, j, ROW_TILE)
            mixer_out = None
        else:
            wqkv, wo = mixer_w16[i]
            lambda_init = 0.8 - 0.6 * math.exp(-0.3 * i)
            a, a_m = _attn_mixer(hx, hm, ln_mix, wqkv, attn_lambda, attn_subln_g,
                                 i, j, ROW_TILE, Q_TILE, lambda_init)
            mixer_out = (a.reshape(bsz * seq, D_MODEL), a_m, ln_mix, wo)
        hx, hm = run_ffn(2 * i + 1, hx, hm, mixer_out=mixer_out)
    return hx
```

```python
import functools
import math

import jax
import jax.numpy as jnp
from jax import lax
from jax.experimental import pallas as pl
from jax.experimental.pallas import tpu as pltpu

D_MODEL = 1024
DEPTH = 4
N_MIXERS = 2
N_META = 16
D_FF = 2816
CONV_WIDTH = 3
HEAD_DIM = 64
V_DIM = 2 * HEAD_DIM
N_HEADS = D_MODEL // V_DIM
ROT_DIM = HEAD_DIM // 4
ROPE_THETA = 500000.0
NORM_EPS = 1e-6
SUBLN_EPS = 1e-5
NEG_INF = -1e30

F32 = jnp.float32
BF16 = jnp.bfloat16

V7X_VMEM_LIMIT_BYTES = 60 * 1024 * 1024
SUBLANES = 8
LANES = 128
CARRY_ROWS = SUBLANES


def _rms(t, g, eps):
    return t * lax.rsqrt(jnp.mean(t * t, axis=-1, keepdims=True) + eps) * g


def _resident(shape):
    zeros = (0,) * len(shape)
    return pl.BlockSpec(shape, lambda *_: zeros, pipeline_mode=pl.Buffered(1))


def _layer_block(shape, layer):
    zeros = (0,) * len(shape)
    return pl.BlockSpec((None,) + tuple(shape), lambda *_: (layer,) + zeros, pipeline_mode=pl.Buffered(1))


def _params(n_grid_dims):
    return pltpu.CompilerParams(
        dimension_semantics=("arbitrary",) * n_grid_dims,
        vmem_limit_bytes=V7X_VMEM_LIMIT_BYTES)


FFN_SUB_ROWS = 128
BF16_ROWS = 16


def _ffn_kernel(*refs, n_casts, mixer_out):
    hx_ref, hm_ref, g_ref, wgu_ref, wdn_ref = refs[:5]
    n_in = 5
    if mixer_out:
        ax_ref, am_ref, gmix_ref, wo_ref = refs[n_in:n_in + 4]
        n_in += 4
    w32_refs = refs[n_in:n_in + n_casts]
    ox_ref, om_ref = refs[n_in + n_casts:n_in + n_casts + 2]
    w16_refs = refs[n_in + n_casts + 2:]
    for w32_ref, w16_ref in zip(w32_refs, w16_refs):
        w16_ref[...] = w32_ref[...].astype(BF16)
    g = g_ref[...]
    sub = FFN_SUB_ROWS
    n_sub = hx_ref.shape[0] // sub

    def ffn(tiles, a_tiles):
        if mixer_out:
            ms = [jnp.dot(a, wo_ref[...], preferred_element_type=F32) for a in a_tiles]
            tiles = [h + _rms(m, gmix_ref[...][1:2], NORM_EPS) for h, m in zip(tiles, ms)]
        hns = [_rms(h, g[0:1], NORM_EPS).astype(BF16) for h in tiles]
        gus = [jnp.dot(hn, wgu_ref[...], preferred_element_type=F32) for hn in hns]
        acts = [(gu[:, :D_FF] * jax.nn.sigmoid(gu[:, :D_FF]) * gu[:, D_FF:]).astype(BF16) for gu in gus]
        fs = [jnp.dot(act, wdn_ref[...], preferred_element_type=F32) for act in acts]
        return [h + 0.5 * _rms(f, g[1:2], NORM_EPS) for h, f in zip(tiles, fs)]

    def real_tiles(x_ref):
        return [x_ref[s * sub:(s + 1) * sub] for s in range(n_sub)]

    def store_real(outs):
        for s, o in enumerate(outs):
            ox_ref[s * sub:(s + 1) * sub] = o

    @pl.when(pl.program_id(0) == 0)
    def _():
        tiles = real_tiles(hx_ref)
        tiles[0] = jnp.concatenate([hm_ref[...], tiles[0]], axis=0)
        a_tiles = None
        if mixer_out:
            a_tiles = real_tiles(ax_ref)
            a_tiles[0] = jnp.concatenate([am_ref[...], a_tiles[0]], axis=0)
        outs = ffn(tiles, a_tiles)
        om_ref[...] = outs[0][:N_META]
        outs[0] = outs[0][N_META:]
        store_real(outs)

    @pl.when(pl.program_id(0) > 0)
    def _():
        store_real(ffn(real_tiles(hx_ref), real_tiles(ax_ref) if mixer_out else None))


def _slab_rows(n_rows, steps):
    for rows in range(BF16_ROWS, n_rows + 1, BF16_ROWS):
        if n_rows % rows == 0 and n_rows // rows <= steps:
            return rows
    raise ValueError((n_rows, steps))


def _ffn(hx, hm, g, wgu, wdn, layer, tm, casts=(), mixer_out=None):
    rows = hx.shape[0]
    steps = rows // tm
    tile = pl.BlockSpec((tm, D_MODEL), lambda i: (i, 0))
    in_specs = [tile, _resident((N_META, D_MODEL)), _layer_block((2, D_MODEL), layer),
                _resident((D_MODEL, 2 * D_FF)), _resident((D_FF, D_MODEL))]
    out_specs = [tile, pl.BlockSpec((N_META, D_MODEL), lambda i: (0, 0))]
    out_shape = [jax.ShapeDtypeStruct((rows, D_MODEL), F32), jax.ShapeDtypeStruct((N_META, D_MODEL), F32)]
    args = [hx, hm, g, wgu, wdn]
    if mixer_out is not None:
        in_specs += [tile, _resident((N_META, D_MODEL)), _layer_block((2, D_MODEL), layer),
                     _resident((D_MODEL, D_MODEL))]
        args += list(mixer_out)
    for w32, index in casts:
        _, n_rows, n_cols = w32.shape
        slab = _slab_rows(n_rows, steps)
        last = n_rows // slab - 1
        in_specs.append(pl.BlockSpec((None, slab, n_cols),
                                     lambda i, index=index, last=last: (index, jnp.minimum(i, last), 0)))
        out_specs.append(pl.BlockSpec((slab, n_cols), lambda i, last=last: (jnp.minimum(i, last), 0)))
        out_shape.append(jax.ShapeDtypeStruct((n_rows, n_cols), BF16))
    return pl.pallas_call(
        functools.partial(_ffn_kernel, n_casts=len(casts), mixer_out=mixer_out is not None),
        grid=(steps,),
        in_specs=in_specs,
        out_specs=out_specs,
        out_shape=out_shape,
        compiler_params=_params(1),
        name="ffn",
    )(*args, *[c[0] for c in casts])


MIXER_SUB_ROWS = 128


def _conv_kernel(h_ref, g_ref, tail_in_ref, win_ref, cw_ref, wout_ref,
                 o_ref, tail_out_ref, carry_ref, *, tm):
    @pl.when(pl.program_id(1) == 0)
    def _():
        carry_ref[...] = tail_in_ref[...]

    g = g_ref[...]
    cw = cw_ref[...]
    n_sub = max(1, tm // MIXER_SUB_ROWS)
    sub = tm // n_sub
    hs = [h_ref[0, s * sub:(s + 1) * sub] for s in range(n_sub)]
    hns = [_rms(h, g[0:1], NORM_EPS).astype(BF16) for h in hs]
    bcxs = [jnp.dot(hn, win_ref[...], preferred_element_type=F32) for hn in hns]
    row = lax.broadcasted_iota(jnp.int32, (sub, 1), 0)
    prev = carry_ref[...]
    ys = []
    for bcx in bcxs:
        u = bcx[:, D_MODEL:2 * D_MODEL] * bcx[:, 2 * D_MODEL:]
        last = prev[CARRY_ROWS - 1:CARRY_ROWS]
        u1 = jnp.where(row == 0, last, pltpu.roll(u, 1, 0))
        u2 = jnp.where(row == 0, prev[CARRY_ROWS - 2:CARRY_ROWS - 1],
                       jnp.where(row == 1, last, pltpu.roll(u, 2, 0)))
        conv = cw[0:1] * u2 + cw[1:2] * u1 + cw[2:3] * u
        ys.append((bcx[:, :D_MODEL] * conv).astype(BF16))
        prev = u[sub - CARRY_ROWS:]
    carry_ref[...] = prev
    tail_out_ref[0] = prev
    ms = [jnp.dot(y, wout_ref[...], preferred_element_type=F32) for y in ys]
    for s in range(n_sub):
        o_ref[0, s * sub:(s + 1) * sub] = hs[s] + _rms(ms[s], g[1:2], NORM_EPS)


def _conv_mixer(h, g, tail_in, win, cw, wout, layer, mixer, tm):
    bsz, seq, _ = h.shape
    return pl.pallas_call(
        functools.partial(_conv_kernel, tm=tm),
        grid=(bsz, seq // tm),
        in_specs=[
            pl.BlockSpec((1, tm, D_MODEL), lambda b, j: (b, j, 0)),
            _layer_block((2, D_MODEL), layer),
            _resident((CARRY_ROWS, D_MODEL)),
            _resident((D_MODEL, 3 * D_MODEL)),
            _layer_block((CONV_WIDTH, D_MODEL), mixer),
            _resident((D_MODEL, D_MODEL)),
        ],
        out_specs=[
            pl.BlockSpec((1, tm, D_MODEL), lambda b, j: (b, j, 0)),
            pl.BlockSpec((1, CARRY_ROWS, D_MODEL), lambda b, j: (b, 0, 0)),
        ],
        out_shape=[
            jax.ShapeDtypeStruct((bsz, seq, D_MODEL), F32),
            jax.ShapeDtypeStruct((bsz, CARRY_ROWS, D_MODEL), F32),
        ],
        scratch_shapes=[pltpu.VMEM((CARRY_ROWS, D_MODEL), F32)],
        compiler_params=_params(2),
        name="conv_mixer",
    )(h, g, tail_in, win, cw, wout)


NT_DIMS = (((1,), (1,)), ((), ()))

ONES_ROWS = 16
VT_ROWS = V_DIM + ONES_ROWS
Q_SCALE = HEAD_DIM ** -0.5 * math.log2(math.e)


def _qkv_kernel(h_ref, g_ref, w_ref, cos_ref, sin_lo_ref, sin_hi_ref, q2_ref, k_ref, vt_ref):
    g = g_ref[...]
    hn = _rms(h_ref[0], g[0:1], NORM_EPS).astype(BF16)
    qk = jnp.dot(hn, w_ref[...], preferred_element_type=F32)
    cos = cos_ref[...]
    sin_lo = sin_lo_ref[...]
    sin_hi = sin_hi_ref[...]
    half = ROT_DIM // 2
    first_component = lax.broadcasted_iota(jnp.int32, (1, LANES), 1) < HEAD_DIM
    for blk in range(2 * N_HEADS):
        x = qk[:, blk * LANES:(blk + 1) * LANES]
        r = x * cos + pltpu.roll(x, LANES - half, 1) * sin_lo + pltpu.roll(x, half, 1) * sin_hi
        if blk < N_HEADS:
            r = (r * Q_SCALE).astype(BF16)
            zero = jnp.zeros_like(r)
            q2_ref[0, 0, :, blk * LANES:(blk + 1) * LANES] = jnp.where(first_component, r, zero)
            q2_ref[0, 1, :, blk * LANES:(blk + 1) * LANES] = jnp.where(first_component, zero, r)
        else:
            k_ref[0, :, (blk - N_HEADS) * LANES:(blk - N_HEADS + 1) * LANES] = r.astype(BF16)
    ones = jnp.ones((ONES_ROWS, qk.shape[0]), BF16)
    for head in range(N_HEADS):
        v = qk[:, 2 * D_MODEL + head * V_DIM:2 * D_MODEL + (head + 1) * V_DIM]
        vt_ref[0, head, :V_DIM, :] = v.T.astype(BF16)
        vt_ref[0, head, V_DIM:, :] = ones


def _qkv_proj(h, g, w, cos, sin_lo, sin_hi, layer, tm):
    bsz, seq, _ = h.shape
    table = pl.BlockSpec((tm, LANES), lambda b, j: (j, 0))
    return pl.pallas_call(
        _qkv_kernel,
        grid=(bsz, seq // tm),
        in_specs=[
            pl.BlockSpec((1, tm, D_MODEL), lambda b, j: (b, j, 0)),
            _layer_block((2, D_MODEL), layer),
            _resident((D_MODEL, 3 * D_MODEL)),
            table, table, table,
        ],
        out_specs=[
            pl.BlockSpec((1, 2, tm, D_MODEL), lambda b, j: (b, 0, j, 0)),
            pl.BlockSpec((1, tm, D_MODEL), lambda b, j: (b, j, 0)),
            pl.BlockSpec((1, N_HEADS, VT_ROWS, tm), lambda b, j: (b, 0, 0, j)),
        ],
        out_shape=[
            jax.ShapeDtypeStruct((bsz, 2, seq, D_MODEL), BF16),
            jax.ShapeDtypeStruct((bsz, seq, D_MODEL), BF16),
            jax.ShapeDtypeStruct((bsz, N_HEADS, VT_ROWS, seq), BF16),
        ],
        compiler_params=_params(2),
        name="qkv_proj",
    )(h, g, w, cos, sin_lo, sin_hi)


SCORE_LOOKAHEAD = 4


def _attn_t_kernel(q2_ref, k_ref, vt_ref, kp_ref, vtp_ref, sg_ref, lam_ref, o_ref, acc_ref, st_ref,
                   *, tq, n_heads, lambda_init):
    seq = k_ref.shape[1]
    ahead = min(SCORE_LOOKAHEAD, n_heads)
    key_i = lax.broadcasted_iota(jnp.int32, (tq, 2 * tq), 0)
    qry_i = lax.broadcasted_iota(jnp.int32, (tq, 2 * tq), 1)
    causal = key_i <= jnp.where(qry_i >= tq, qry_i - tq, qry_i)
    lam_p = lam_ref[...]
    lam = (jnp.exp(jnp.sum(lam_p[0:1] * lam_p[1:2], axis=1, keepdims=True))
           - jnp.exp(jnp.sum(lam_p[2:3] * lam_p[3:4], axis=1, keepdims=True)) + lambda_init)
    sg = sg_ref[...]
    heads = [slice(g * LANES, (g + 1) * LANES) for g in range(n_heads)]

    def q_tile(qi, _):
        q_start = pl.multiple_of(qi * tq, tq)
        qqs = [jnp.concatenate([q2_ref[0, 0, pl.ds(q_start, tq), hs],
                                q2_ref[0, 1, pl.ds(q_start, tq), hs]], axis=0) for hs in heads]

        def scores(g, start):
            return lax.dot_general(k_ref[0, pl.ds(start, tq), heads[g]], qqs[g], NT_DIMS,
                                   preferred_element_type=F32)

        def first_scores(g):
            k_all = jnp.concatenate([k_ref[0, pl.ds(q_start, tq), heads[g]], kp_ref[:, heads[g]]], axis=0)
            st_all = lax.dot_general(k_all, qqs[g], NT_DIMS, preferred_element_type=F32)
            return st_all[tq:], st_all[:tq]

        def fold(start, next_start, maxes):
            first = maxes is None
            pending = [first_scores(g) for g in range(ahead)] if first else [None] * ahead
            new_maxes = []
            for g in range(n_heads):
                st = pending.pop(0)
                if st is None:
                    st = st_ref[g]
                upcoming = g + ahead
                if upcoming < n_heads:
                    pending.append(first_scores(upcoming) if first else scores(upcoming, start))
                else:
                    st_ref[upcoming - n_heads] = scores(upcoming - n_heads, next_start)
                vt = vt_ref[0, g, :, pl.ds(start, tq)]
                if first:
                    st_p, st = st
                    st = jnp.where(causal, st, NEG_INF)
                    m_new = jnp.maximum(jnp.max(st, axis=0, keepdims=True), jnp.max(st_p, axis=0, keepdims=True))
                    p_all = jnp.concatenate([jnp.exp2(st - m_new).astype(BF16),
                                             jnp.exp2(st_p - m_new).astype(BF16)], axis=0)
                    vt_all = jnp.concatenate([vt, vtp_ref[g]], axis=1)
                    acc_ref[g] = jnp.dot(vt_all, p_all, preferred_element_type=F32)
                else:
                    m_new = jnp.maximum(maxes[g], jnp.max(st, axis=0, keepdims=True))
                    alpha = jnp.exp2(maxes[g] - m_new)
                    pv = jnp.dot(vt, jnp.exp2(st - m_new).astype(BF16), preferred_element_type=F32)
                    acc_ref[g] = alpha * acc_ref[g] + pv
                new_maxes.append(m_new)
            return tuple(new_maxes)

        def block(kb, maxes):
            start = pl.multiple_of(kb * tq, tq)
            return fold(start, start + tq, maxes)

        lax.fori_loop(0, qi, block, fold(q_start, 0, None))

        for g, hs in enumerate(heads):
            acc = acc_ref[g]
            pv = acc[:V_DIM] * (1.0 / acc[V_DIM:V_DIM + 1])
            o = pv[:, :tq] - lam * pv[:, tq:]
            o = o * lax.rsqrt(jnp.mean(o * o, axis=0, keepdims=True) + SUBLN_EPS) * sg * (1.0 - lambda_init)
            o_ref[0, pl.ds(q_start, tq), hs] = o.T.astype(BF16)
        return 0

    lax.fori_loop(0, seq // tq, q_tile, 0)


def _attention_t(q2, k, vt, lam_p, subln_g, prefix_k, prefix_vt, mixer, tq, n_heads, lambda_init):
    bsz, seq, _ = k.shape
    n_prefix = prefix_k.shape[0]
    groups = N_HEADS // n_heads
    width = n_heads * LANES
    return pl.pallas_call(
        functools.partial(_attn_t_kernel, tq=tq, n_heads=n_heads, lambda_init=lambda_init),
        grid=(bsz, groups),
        in_specs=[
            pl.BlockSpec((1, 2, seq, width), lambda b, h: (b, 0, 0, h)),
            pl.BlockSpec((1, seq, width), lambda b, h: (b, 0, h)),
            pl.BlockSpec((1, n_heads, VT_ROWS, seq), lambda b, h: (b, h, 0, 0)),
            pl.BlockSpec((n_prefix, width), lambda b, h: (0, h)),
            pl.BlockSpec((n_heads, VT_ROWS, n_prefix), lambda b, h: (h, 0, 0)),
            _layer_block((V_DIM, 1), mixer),
            _layer_block((4, HEAD_DIM), mixer),
        ],
        out_specs=pl.BlockSpec((1, seq, width), lambda b, h: (b, 0, h)),
        out_shape=jax.ShapeDtypeStruct((bsz, seq, D_MODEL), BF16),
        scratch_shapes=[pltpu.VMEM((n_heads, VT_ROWS, 2 * tq), F32),
                        pltpu.VMEM((min(SCORE_LOOKAHEAD, n_heads), tq, 2 * tq), F32)],
        compiler_params=_params(2),
        name="diff_attention",
    )(q2, k, vt, prefix_k, prefix_vt, subln_g.reshape(-1, V_DIM, 1), lam_p)


def _attn_small_kernel(lam_ref, sg_ref, q2_ref, k_ref, v_ref, o_ref, *, lambda_init):
    tq = k_ref.shape[0]
    qq = jnp.concatenate([q2_ref[0], q2_ref[1]], axis=0)
    s = lax.dot_general(qq, k_ref[...], NT_DIMS, preferred_element_type=F32)
    r = lax.broadcasted_iota(jnp.int32, (2 * tq, tq), 0)
    c = lax.broadcasted_iota(jnp.int32, (2 * tq, tq), 1)
    s = jnp.where(c <= jnp.where(r >= tq, r - tq, r), s, NEG_INF)
    p = jnp.exp2(s - jnp.max(s, axis=1, keepdims=True))
    l = jnp.sum(p, axis=1, keepdims=True)
    pv = jnp.dot(p.astype(BF16), v_ref[...], preferred_element_type=F32) / l
    lam_p = lam_ref[...]
    lam = (jnp.exp(jnp.sum(lam_p[0:1] * lam_p[1:2], axis=1, keepdims=True))
           - jnp.exp(jnp.sum(lam_p[2:3] * lam_p[3:4], axis=1, keepdims=True)) + lambda_init)
    o = pv[:tq] - lam * pv[tq:]
    o_ref[...] = (_rms(o, sg_ref[...], SUBLN_EPS) * (1.0 - lambda_init)).astype(BF16)


def _attention_small(q2, k, v, lam_p, subln_g, mixer, lambda_init):
    seq = k.shape[0]
    return pl.pallas_call(
        functools.partial(_attn_small_kernel, lambda_init=lambda_init),
        grid=(N_HEADS,),
        in_specs=[
            _layer_block((4, HEAD_DIM), mixer),
            _layer_block((1, V_DIM), mixer),
            pl.BlockSpec((2, seq, LANES), lambda h: (0, 0, h)),
            pl.BlockSpec((seq, LANES), lambda h: (0, h)),
            pl.BlockSpec((seq, LANES), lambda h: (0, h)),
        ],
        out_specs=pl.BlockSpec((seq, LANES), lambda h: (0, h)),
        out_shape=jax.ShapeDtypeStruct((seq, D_MODEL), BF16),
        compiler_params=_params(1),
        name="diff_attention_meta",
    )(lam_p, subln_g.reshape(-1, 1, V_DIM), q2, k, v)


def _rope_tables(first_pos, n_pos):
    half = ROT_DIM // 2
    pos = jnp.arange(first_pos, first_pos + n_pos, dtype=F32)
    inv_freq = ROPE_THETA ** (-jnp.arange(0, ROT_DIM, 2, dtype=F32) / ROT_DIM)
    ang = pos[:, None] * inv_freq[None, :]
    cos, sin = jnp.cos(ang), jnp.sin(ang)
    ones = jnp.ones((n_pos, HEAD_DIM - ROT_DIM), F32)
    zeros = jnp.zeros((n_pos, HEAD_DIM - half), F32)
    cos64 = jnp.concatenate([cos, cos, ones], axis=1)
    lo64 = jnp.concatenate([-sin, zeros], axis=1)
    hi64 = jnp.concatenate([jnp.zeros((n_pos, half), F32), sin, ones * 0.0], axis=1)
    return tuple(jnp.concatenate([t, t], axis=1) for t in (cos64, lo64, hi64))


def _attn_mixer(hx, hm, g, wqkv, lam_p, subln_g, layer, mixer, tm, tq, lambda_init):
    bsz, seq, _ = hx.shape
    q2_m, k_m, vt_m = (t[0] for t in _qkv_proj(hm, g, wqkv, *_rope_tables(0, N_META), layer, N_META))
    v_m = jnp.transpose(vt_m[:, :V_DIM, :], (2, 0, 1)).reshape(N_META, D_MODEL)
    a_m = _attention_small(q2_m, k_m, v_m, lam_p, subln_g, mixer, lambda_init)
    q2, k, vt = _qkv_proj(hx, g, wqkv, *_rope_tables(N_META, seq), layer, tm)
    a = _attention_t(q2, k, vt, lam_p, subln_g, k_m, vt_m, mixer, tq, HEADS_PER_STEP, lambda_init)
    return a, a_m


ROW_TILE = 512
FFN_ROW_TILE = 512
Q_TILE = 256
HEADS_PER_STEP = 8


def kernel(x, meta_tokens, ln_ffn1, ffn1_w_gu, ffn1_w_down, ln_mix, conv_w_in, conv_w, conv_w_out,
           attn_w_qkv, attn_lambda, attn_subln_g, attn_w_o, ln_ffn2, ffn2_w_gu, ffn2_w_down):
    bsz, seq, d = x.shape
    assert d == D_MODEL and seq % ROW_TILE == 0 and seq % Q_TILE == 0 and (bsz * seq) % FFN_ROW_TILE == 0
    hx = x
    hm = meta_tokens.astype(x.dtype)[None]

    ffns = [(ln, wgu, wdn, i) for i in range(DEPTH)
            for ln, wgu, wdn in ((ln_ffn1, ffn1_w_gu, ffn1_w_down), (ln_ffn2, ffn2_w_gu, ffn2_w_down))]
    ffn_w16 = [ffns[0][1][0].astype(BF16), ffns[0][2][0].astype(BF16)]
    mixer_w32 = [[(conv_w_in, i // N_MIXERS), (conv_w_out, i // N_MIXERS)] if i % N_MIXERS == 0 else
                 [(attn_w_qkv, i // N_MIXERS), (attn_w_o, i // N_MIXERS)] for i in range(DEPTH)]
    mixer_w16 = {}

    def run_ffn(n, hx, hm, mixer_out=None):
        nonlocal ffn_w16
        ln, _, _, layer = ffns[n]
        mixer_layer = 0 if n == 0 else (n + 1) // 2 if n % 2 == 1 and (n + 1) // 2 < DEPTH else None
        casts = list(mixer_w32[mixer_layer]) if mixer_layer is not None else []
        n_mixer = len(casts)
        if n + 1 < len(ffns):
            _, wgu32, wdn32, nxt = ffns[n + 1]
            casts += [(wgu32, nxt), (wdn32, nxt)]
        hx, hm, *w16 = _ffn(hx.reshape(bsz * seq, D_MODEL), hm[0], ln, *ffn_w16, layer, FFN_ROW_TILE, casts,
                            mixer_out)
        if mixer_layer is not None:
            mixer_w16[mixer_layer] = w16[:n_mixer]
        ffn_w16 = w16[n_mixer:]
        return hx.reshape(bsz, seq, D_MODEL), hm[None]

    no_history = jnp.zeros((CARRY_ROWS, D_MODEL), F32)
    for i in range(DEPTH):
        j = i // N_MIXERS
        hx, hm = run_ffn(2 * i, hx, hm)
        if i % N_MIXERS == 0:
            win, wout = mixer_w16[i]
            hm, meta_tail = _conv_mixer(hm, ln_mix, no_history, win, conv_w, wout, i, j, N_META)
            hx, _ = _conv_mixer(hx, ln_mix, meta_tail[0], win, conv_w, wout, i, j, ROW_TILE)
            mixer_out = None
        else:
            wqkv, wo = mixer_w16[i]
            lambda_init = 0.8 - 0.6 * math.exp(-0.3 * i)
            a, a_m = _attn_mixer(hx, hm, ln_mix, wqkv, attn_lambda, attn_subln_g,
                                 i, j, ROW_TILE, Q_TILE, lambda_init)
            mixer_out = (a.reshape(bsz * seq, D_MODEL), a_m, ln_mix, wo)
        hx, hm = run_ffn(2 * i + 1, hx, hm, mixer_out=mixer_out)
    return hx
```
